```python
import math
import jax, jax.numpy as jnp
from jax import lax
import numpy as np

D_MODEL = 1024
BATCH = 2
SEQ = 16384
DEPTH = 4

GRID_W = 64
CTX_LEN = 256
D_MIX = D_MODEL
POOL_WINDOWS = (2, 4, 8, 16)
D_POOL = D_MIX // 4
POOL_GROUP = D_POOL // len(POOL_WINDOWS)
SWA_HEADS = 4
SWA_KV_HEADS = 2
SWA_HEAD_DIM = 64
SWA_WINDOW = 128
BLOCK = 128
MLA_HEADS = 4
MLA_NOPE = 64
MLA_ROPE = 32
MLA_V = 64
MLA_Q_RANK = 256
MLA_KV_RANK = 128
SGU_HEADS = 4
SGU_WIDTH = D_MIX // 4
SGU_HEAD_DIM = SGU_WIDTH // SGU_HEADS
SGU_CHUNK = 128
D_FF = 4 * D_MODEL
ROPE_BASE = 10000.0
DN_ALPHA = (2 * DEPTH) ** 0.25
DN_BETA = (8 * DEPTH) ** -0.25
N_MOD = 6
EPS = 1e-6

W_POOL = D_POOL
W_SWA_Q = SWA_HEADS * SWA_HEAD_DIM
W_MLA_CQ = MLA_Q_RANK
W_SGU = 2 * SGU_WIDTH
W_SWA_KV = SWA_KV_HEADS * SWA_HEAD_DIM
W_MLA_CKV = MLA_KV_RANK
W_MLA_KR = MLA_ROPE
C_POOL = 0
C_SWA_Q = C_POOL + W_POOL
C_MLA_CQ = C_SWA_Q + W_SWA_Q
C_SGU = C_MLA_CQ + W_MLA_CQ
C_KV = C_SGU + W_SGU
C_SWA_K = C_KV
C_SWA_V = C_SWA_K + W_SWA_KV
C_MLA_CKV = C_SWA_V + W_SWA_KV
C_MLA_KR = C_MLA_CKV + W_MLA_CKV
D_IN = C_MLA_KR + W_MLA_KR

kernel_name = 'hybrid_pool_swa_mla_sgu_dit_trunk'

F32 = jnp.float32


def cols(z, start, width):
    return z[..., start:start + width]


def layer_norm(x):
    xf = x.astype(F32)
    mu = jnp.mean(xf, axis=-1, keepdims=True)
    var = jnp.mean(jnp.square(xf - mu), axis=-1, keepdims=True)
    return ((xf - mu) * lax.rsqrt(var + EPS)).astype(x.dtype)


def layer_norm_affine(x, g, b):
    return layer_norm(x) * g + b


def rms_norm(x, g):
    xf = x.astype(F32)
    y = xf * lax.rsqrt(jnp.mean(jnp.square(xf), axis=-1, keepdims=True) + EPS)
    return y.astype(x.dtype) * g


def modulate(x, shift, scale):
    return layer_norm(x) * (1 + scale) + shift


def axial_angles(n, d_rot):
    rows = n // GRID_W
    row = jnp.repeat(jnp.arange(rows), GRID_W).astype(F32)
    col = jnp.tile(jnp.arange(GRID_W), rows).astype(F32)
    d_ax = d_rot // 2
    inv = ROPE_BASE ** (-jnp.arange(0, d_ax, 2, dtype=F32) / d_ax)
    return row[:, None] * inv, col[:, None] * inv


def _rotate(x, ang):
    xf = x.astype(F32)
    x1, x2 = jnp.split(xf, 2, axis=-1)
    cos, sin = jnp.cos(ang), jnp.sin(ang)
    return jnp.concatenate([x1 * cos - x2 * sin, x2 * cos + x1 * sin], axis=-1).astype(x.dtype)


def apply_axial_rope(x, angles):
    xr, xc = jnp.split(x, 2, axis=-1)
    return jnp.concatenate([_rotate(xr, angles[0]), _rotate(xc, angles[1])], axis=-1)


def pool_mixer(za, pool_w, pool_scale):
    b, n, _ = za.shape
    g_n = len(POOL_WINDOWS)
    xg = za.reshape(b, n, g_n, POOL_GROUP)
    xf = xg.astype(F32)
    cs = jnp.concatenate([jnp.zeros((b, 1, g_n, POOL_GROUP), F32), jnp.cumsum(xf, axis=1)], axis=1)
    t = jnp.arange(n)
    means = []
    for gi, w in enumerate(POOL_WINDOWS):
        lo = jnp.clip(t - w // 2, 0, n)
        hi = jnp.clip(t + w // 2, 0, n)
        s = cs[:, hi, gi] - cs[:, lo, gi]
        means.append(s / (hi - lo).astype(F32)[None, :, None])
    d = (jnp.stack(means, axis=2) - xf).astype(za.dtype)
    y = jnp.einsum('bngc,gce->bnge', d, pool_w).reshape(b, n, D_POOL)
    return y * pool_scale


def swa_latent(q, k, v, k_c, v_c, sink):
    b, n, _, d = q.shape
    nb = n // BLOCK
    grp = SWA_HEADS // SWA_KV_HEADS
    scale = d ** -0.5
    qb = q.reshape(b, nb, BLOCK, SWA_KV_HEADS, grp, d)
    pad = ((0, 0), (BLOCK, BLOCK), (0, 0), (0, 0))
    kp = jnp.pad(k, pad).reshape(b, nb + 2, BLOCK, SWA_KV_HEADS, d)
    vp = jnp.pad(v, pad).reshape(b, nb + 2, BLOCK, SWA_KV_HEADS, d)
    kband = jnp.concatenate([kp[:, :-2], kp[:, 1:-1], kp[:, 2:]], axis=2)
    vband = jnp.concatenate([vp[:, :-2], vp[:, 1:-1], vp[:, 2:]], axis=2)
    s_band = jnp.einsum('bnqhgd,bnkhd->bnhgqk', qb, kband, preferred_element_type=F32) * scale
    qpos = jnp.arange(nb)[:, None] * BLOCK + jnp.arange(BLOCK)[None, :]
    kpos = (jnp.arange(nb)[:, None] - 1) * BLOCK + jnp.arange(3 * BLOCK)[None, :]
    rel = kpos[:, None, :] - qpos[:, :, None]
    valid = (jnp.abs(rel) <= SWA_WINDOW) & (kpos[:, None, :] >= 0) & (kpos[:, None, :] < n)
    s_band = jnp.where(valid[None, :, None, None], s_band, -jnp.inf)
    s_ctx = jnp.einsum('bnqhgd,bchd->bnhgqc', qb, k_c, preferred_element_type=F32) * scale
    sink_col = jnp.broadcast_to(sink.astype(F32).reshape(SWA_KV_HEADS, grp)[:, :, None, None],
                                s_band.shape[:-1] + (1,))
    p = jax.nn.softmax(jnp.concatenate([s_band, s_ctx, sink_col], axis=-1), axis=-1)
    nk = 3 * BLOCK
    n_ctx = k_c.shape[1]
    o = (jnp.einsum('bnhgqk,bnkhd->bnqhgd', p[..., :nk].astype(v.dtype), vband)
         + jnp.einsum('bnhgqc,bchd->bnqhgd', p[..., nk:nk + n_ctx].astype(v.dtype), v_c))
    return o.reshape(b, n, SWA_HEADS * d)


def swa_context(q_c, k_c, v_c, sink):
    b, n, _, d = q_c.shape
    grp = SWA_HEADS // SWA_KV_HEADS
    qg = q_c.reshape(b, n, SWA_KV_HEADS, grp, d)
    s = jnp.einsum('bqhgd,bkhd->bhgqk', qg, k_c, preferred_element_type=F32) * d ** -0.5
    sink_col = jnp.broadcast_to(sink.astype(F32).reshape(SWA_KV_HEADS, grp)[None, :, :, None, None],
                                s.shape[:-1] + (1,))
    p = jax.nn.softmax(jnp.concatenate([s, sink_col], axis=-1), axis=-1)[..., :n]
    o = jnp.einsum('bhgqk,bkhd->bqhgd', p.astype(v_c.dtype), v_c)
    return o.reshape(b, n, SWA_HEADS * d)


def mla_expand_q(cq, g, w_uq):
    b, n, _ = cq.shape
    q = (rms_norm(cq, g) @ w_uq).reshape(b, n, MLA_HEADS, MLA_NOPE + MLA_ROPE)
    return q[..., :MLA_NOPE], q[..., MLA_NOPE:]


def mla_expand_kv(ckv, g, w_ukv):
    b, n, _ = ckv.shape
    kv = (rms_norm(ckv, g) @ w_ukv).reshape(b, n, MLA_HEADS, MLA_NOPE + MLA_V)
    return kv[..., :MLA_NOPE], kv[..., MLA_NOPE:]


def kv_sources(zkv, kv_norm, w_ukv):
    b, n, _ = zkv.shape
    k = cols(zkv, C_SWA_K - C_KV, W_SWA_KV).reshape(b, n, SWA_KV_HEADS, SWA_HEAD_DIM)
    v = cols(zkv, C_SWA_V - C_KV, W_SWA_KV).reshape(b, n, SWA_KV_HEADS, SWA_HEAD_DIM)
    kn, vm = mla_expand_kv(cols(zkv, C_MLA_CKV - C_KV, W_MLA_CKV), kv_norm, w_ukv)
    kr = cols(zkv, C_MLA_KR - C_KV, W_MLA_KR)
    return k, v, kn, kr, vm


def mla_latent(qn, qr, kn, kr, vm, kn_c, kr_c, vm_c):
    b, n, h, _ = qn.shape
    nb = n // BLOCK
    scale = (MLA_NOPE + MLA_ROPE) ** -0.5
    qn_b = qn.reshape(b, nb, BLOCK, h, MLA_NOPE).transpose(1, 0, 2, 3, 4)
    qr_b = qr.reshape(b, nb, BLOCK, h, MLA_ROPE).transpose(1, 0, 2, 3, 4)

    def one_block(args):
        qn_i, qr_i = args
        s_lat = (jnp.einsum('bqhd,bkhd->bhqk', qn_i, kn, preferred_element_type=F32)
                 + jnp.einsum('bqhr,bkr->bhqk', qr_i, kr, preferred_element_type=F32)) * scale
        s_ctx = (jnp.einsum('bqhd,bkhd->bhqk', qn_i, kn_c, preferred_element_type=F32)
                 + jnp.einsum('bqhr,bkr->bhqk', qr_i, kr_c, preferred_element_type=F32)) * scale
        p = jax.nn.softmax(jnp.concatenate([s_lat, s_ctx], axis=-1), axis=-1)
        return (jnp.einsum('bhqk,bkhd->bqhd', p[..., :n].astype(vm.dtype), vm)
                + jnp.einsum('bhqk,bkhd->bqhd', p[..., n:].astype(vm.dtype), vm_c))

    o = lax.map(one_block, (qn_b, qr_b))
    return o.transpose(1, 0, 2, 3, 4).reshape(b, n, h * MLA_V)


def mla_context(qn, qr, kn, kr, vm):
    b, n, h, _ = qn.shape
    scale = (MLA_NOPE + MLA_ROPE) ** -0.5
    s = (jnp.einsum('bqhd,bkhd->bhqk', qn, kn, preferred_element_type=F32)
         + jnp.einsum('bqhr,bkr->bhqk', qr, kr, preferred_element_type=F32)) * scale
    p = jax.nn.softmax(s, axis=-1).astype(vm.dtype)
    return jnp.einsum('bhqk,bkhd->bqhd', p, vm).reshape(b, n, h * MLA_V)


def sgu_mixer(zd, norm_g, norm_b, w_s, b_s):
    b, n, _ = zd.shape
    z = jax.nn.gelu(zd, approximate=False)
    u, v = jnp.split(z, 2, axis=-1)
    v = layer_norm_affine(v, norm_g, norm_b)
    nc = n // SGU_CHUNK
    vc = v.reshape(b, nc, SGU_CHUNK, SGU_HEADS, SGU_HEAD_DIM)
    mixed = jnp.einsum('hpq,bcqhd->bcphd', w_s, vc) + b_s.T[None, None, :, :, None]
    return u * mixed.reshape(b, n, SGU_WIDTH)


def channel_mlp(h, w1, w2):
    return jnp.square(jax.nn.relu(h @ w1)) @ w2


def setup_inputs(seed: int = 0) -> dict:
    key = jax.random.key(seed)
    ks = jax.random.split(key, 32)
    L, D = DEPTH, D_MODEL

    def nrm(i, shape, s):
        return jax.random.normal(ks[i], shape, F32) * s

    gate_offset = jnp.repeat(jnp.array([0., 0., 1., 0., 0., 1.], F32), D)
    return {
        'x': nrm(0, (BATCH, SEQ, D), 1.0),
        'c': nrm(1, (BATCH, D), 1.0),
        'ctx': nrm(2, (BATCH, CTX_LEN, D), 1.0),
        'c_ctx': nrm(3, (D,), 1.0),
        'w_ada': nrm(4, (L, D, N_MOD * D), 0.25 * D ** -0.5),
        'b_ada': nrm(5, (L, N_MOD * D), 0.02) + gate_offset,
        'w_in': nrm(6, (L, D, D_IN), D ** -0.5),
        'pool_w': nrm(7, (L, len(POOL_WINDOWS), POOL_GROUP, POOL_GROUP), POOL_GROUP ** -0.5),
        'pool_scale': 1.0 + nrm(8, (L, D_POOL), 0.1),
        'swa_sink': nrm(9, (L, SWA_HEADS), 0.5),
        'mla_q_norm': 1.0 + nrm(10, (L, MLA_Q_RANK), 0.02),
        'mla_w_uq': nrm(11, (L, MLA_Q_RANK, MLA_HEADS * (MLA_NOPE + MLA_ROPE)), MLA_Q_RANK ** -0.5),
        'mla_kv_norm': 1.0 + nrm(12, (L, MLA_KV_RANK), 0.02),
        'mla_w_ukv': nrm(13, (L, MLA_KV_RANK, MLA_HEADS * (MLA_NOPE + MLA_V)), MLA_KV_RANK ** -0.5),
        'sgu_norm_g': 1.0 + nrm(14, (L, SGU_WIDTH), 0.02),
        'sgu_norm_b': nrm(15, (L, SGU_WIDTH), 0.02),
        'sgu_w': nrm(16, (L, SGU_HEADS, SGU_CHUNK, SGU_CHUNK), SGU_CHUNK ** -0.5),
        'sgu_b': 1.0 + nrm(17, (L, SGU_HEADS, SGU_CHUNK), 0.02),
        'w_out': nrm(18, (L, D_MIX, D), D_MIX ** -0.5 * DN_BETA),
        'ln1_g': 1.0 + nrm(19, (L, D), 0.02),
        'ln1_b': nrm(20, (L, D), 0.02),
        'w_ff1': nrm(21, (L, D, D_FF), D ** -0.5),
        'w_ff2': nrm(22, (L, D_FF, D), D_FF ** -0.5 * DN_BETA),
        'ln2_g': 1.0 + nrm(23, (L, D), 0.02),
        'ln2_b': nrm(24, (L, D), 0.02),
    }


def reference(x, c, ctx, c_ctx, w_ada, b_ada, w_in, pool_w, pool_scale, swa_sink,
              mla_q_norm, mla_w_uq, mla_kv_norm, mla_w_ukv, sgu_norm_g, sgu_norm_b,
              sgu_w, sgu_b, w_out, ln1_g, ln1_b, w_ff1, w_ff2, ln2_g, ln2_b):
    b, n, _ = x.shape
    n_ctx = ctx.shape[1]
    swa_ang = axial_angles(n, SWA_HEAD_DIM)
    swa_ang_h = (swa_ang[0][:, None, :], swa_ang[1][:, None, :])
    mla_ang = axial_angles(n, MLA_ROPE)
    mla_ang_h = (mla_ang[0][:, None, :], mla_ang[1][:, None, :])
    s_c = jax.nn.silu(c)
    s_cc = jax.nn.silu(c_ctx)

    for l in range(DEPTH):
        last = l == DEPTH - 1
        m = jnp.split((s_c @ w_ada[l] + b_ada[l])[:, None, :], N_MOD, axis=-1)
        n_mod_c = 2 if last else N_MOD
        mc = jnp.split(s_cc @ w_ada[l][:, :n_mod_c * D_MODEL] + b_ada[l][:n_mod_c * D_MODEL], n_mod_c)

        hc = modulate(ctx, mc[0], mc[1])
        if last:
            zkv_c = hc @ w_in[l][:, C_KV:]
        else:
            zc = hc @ w_in[l]
            zkv_c = zc[..., C_KV:]
        k_c, v_c, kn_c, kr_c, vm_c = kv_sources(zkv_c, mla_kv_norm[l], mla_w_ukv[l])

        h = modulate(x, m[0], m[1])
        z = h @ w_in[l]
        k, v, kn, kr, vm = kv_sources(z[..., C_KV:], mla_kv_norm[l], mla_w_ukv[l])
        k = apply_axial_rope(k, swa_ang_h)
        kr = apply_axial_rope(kr, mla_ang)
        q = apply_axial_rope(cols(z, C_SWA_Q, W_SWA_Q).reshape(b, n, SWA_HEADS, SWA_HEAD_DIM), swa_ang_h)
        qn, qr = mla_expand_q(cols(z, C_MLA_CQ, W_MLA_CQ), mla_q_norm[l], mla_w_uq[l])
        qr = apply_axial_rope(qr, mla_ang_h)
        y = jnp.concatenate([
            pool_mixer(cols(z, C_POOL, W_POOL), pool_w[l], pool_scale[l]),
            swa_latent(q, k, v, k_c, v_c, swa_sink[l]),
            mla_latent(qn, qr, kn, kr, vm, kn_c, kr_c, vm_c),
            sgu_mixer(cols(z, C_SGU, W_SGU), sgu_norm_g[l], sgu_norm_b[l], sgu_w[l], sgu_b[l]),
        ], axis=-1) @ w_out[l]
        x = layer_norm_affine(DN_ALPHA * x + m[2] * y, ln1_g[l], ln1_b[l])
        f = channel_mlp(modulate(x, m[3], m[4]), w_ff1[l], w_ff2[l])
        x = layer_norm_affine(DN_ALPHA * x + m[5] * f, ln2_g[l], ln2_b[l])

        if not last:
            q_c = cols(zc, C_SWA_Q, W_SWA_Q).reshape(b, n_ctx, SWA_HEADS, SWA_HEAD_DIM)
            qn_c, qr_c = mla_expand_q(cols(zc, C_MLA_CQ, W_MLA_CQ), mla_q_norm[l], mla_w_uq[l])
            yc = jnp.concatenate([
                pool_mixer(cols(zc, C_POOL, W_POOL), pool_w[l], pool_scale[l]),
                swa_context(q_c, k_c, v_c, swa_sink[l]),
                mla_context(qn_c, qr_c, kn_c, kr_c, vm_c),
                sgu_mixer(cols(zc, C_SGU, W_SGU), sgu_norm_g[l], sgu_norm_b[l], sgu_w[l], sgu_b[l]),
            ], axis=-1) @ w_out[l]
            ctx = layer_norm_affine(DN_ALPHA * ctx + mc[2] * yc, ln1_g[l], ln1_b[l])
            fc = channel_mlp(modulate(ctx, mc[3], mc[4]), w_ff1[l], w_ff2[l])
            ctx = layer_norm_affine(DN_ALPHA * ctx + mc[5] * fc, ln2_g[l], ln2_b[l])
    return x
```

```python
import functools
import math

import jax
import jax.numpy as jnp
import numpy as np
from jax import lax
from jax.experimental import pallas as pl
from jax.experimental.pallas import tpu as pltpu

F32 = jnp.float32
BF16 = jnp.bfloat16

GRID_W = 64
POOL_WINDOWS = (2, 4, 8, 16)
POOL_GROUP = 64
D_POOL = 256
SWA_HEADS = 4
SWA_KV_HEADS = 2
SWA_HEAD_DIM = 64
SWA_WINDOW = 128
BLOCK = 128
MLA_HEADS = 4
MLA_NOPE = 64
MLA_ROPE = 32
MLA_V = 64
MLA_Q_RANK = 256
MLA_KV_RANK = 128
SGU_HEADS = 4
SGU_WIDTH = 256
SGU_HEAD_DIM = 64
SGU_CHUNK = 128
ROPE_BASE = 10000.0
N_MOD = 6
EPS = 1e-6

LANE = 128
ROW_TILE = 512
MIX_ROWS = 512
HALO = 8
MLA_TQ = 512
MLA_TK = 512
FF_CHUNK = 1024
VMEM_LIMIT = 56 * 1024 * 1024

HEAD_PAD = LANE

G_POOL = (0, 256)
G_Q = (256, 512)
G_QSW = (768, 512)
G_CQ = (1280, 256)
G_SGU = (1536, 512)
G_K = (2048, 256)
G_KSW = (2304, 256)
G_V = (2560, 128)
G_CKV = (2688, 128)
G_KR = (2816, 128)
G_KRSW = (2944, 128)
D_IN_EXT = 3072


def _ln(x):
    mu = jnp.mean(x, axis=-1, keepdims=True)
    xc = x - mu
    var = jnp.mean(xc * xc, axis=-1, keepdims=True)
    return xc * lax.rsqrt(var + EPS)


def _dot(a, b):
    return jnp.dot(a, b, preferred_element_type=F32)


def _dot_nt(a, b):
    return lax.dot_general(a, b, (((1,), (1,)), ((), ())), preferred_element_type=F32)


def _gelu(x):
    return 0.5 * x * (1.0 + lax.erf(x * (1.0 / math.sqrt(2.0))))


def _mod_kernel(c_ref, w_ref, b_ref, o_ref):
    c = c_ref[...]
    s = c * (1.0 / (1.0 + jnp.exp(-c)))
    o_ref[0] = jnp.dot(s, w_ref[0], preferred_element_type=F32,
                       precision=lax.Precision.HIGHEST) + b_ref[0]


def _modulation(cond, w_ada, b_ada):
    L, D, D6 = w_ada.shape
    nc = D6 // D
    return pl.pallas_call(
        _mod_kernel,
        grid=(L, nc),
        in_specs=[
            pl.BlockSpec((8, D), lambda l, j: (0, 0)),
            pl.BlockSpec((1, D, D), lambda l, j: (l, 0, j)),
            pl.BlockSpec((1, 1, D), lambda l, j: (l, 0, j)),
        ],
        out_specs=pl.BlockSpec((1, 8, D), lambda l, j: (l, 0, j)),
        out_shape=jax.ShapeDtypeStruct((L, 8, D6), F32),
        compiler_params=pltpu.CompilerParams(
            dimension_semantics=("arbitrary", "arbitrary"), vmem_limit_bytes=VMEM_LIMIT),
        name="adaln_modulation",
    )(cond, w_ada, b_ada.reshape(L, 1, D6))


def _inproj_kernel(x_ref, mod_ref, cq_ref, sq_ref, cm_ref, sm_ref, win_ref,
                   qg_ref, wuq_ref, kvg_ref, wkn_ref, wvt_ref,
                   sg_ref, sb_ref, sw_ref, sbias_ref,
                   zpool_ref, qs_ref, ks_ref, vs_ref, qm_ref, km_ref, vmt_ref, ysgu_ref):
    x = x_ref[...]
    shift = mod_ref[0, 0:1, :]
    scale = mod_ref[0, 1:2, :]
    h = (_ln(x) * (1.0 + scale) + shift).astype(BF16)

    def proj(g):
        return _dot(h, win_ref[:, g[0]:g[0] + g[1]])

    zpool_ref[...] = proj(G_POOL)

    cos_q = cq_ref[...]
    sin_q = sq_ref[...]
    qs_ref[...] = (proj(G_Q) * cos_q + proj(G_QSW) * sin_q).astype(BF16)
    ks_ref[...] = (proj(G_K) * cos_q[:, :G_K[1]] + proj(G_KSW) * sin_q[:, :G_K[1]]).astype(BF16)
    vs_ref[...] = proj(G_V).astype(BF16)

    cos_m = cm_ref[...]
    sin_m = sm_ref[...]
    cq = proj(G_CQ)
    cqn = cq * lax.rsqrt(jnp.mean(cq * cq, axis=-1, keepdims=True) + EPS) * qg_ref[...]
    q2 = _dot(cqn.astype(BF16), wuq_ref[...])
    hw = MLA_HEADS * HEAD_PAD
    cos_m4 = jnp.concatenate([cos_m] * MLA_HEADS, axis=-1)
    sin_m4 = jnp.concatenate([sin_m] * MLA_HEADS, axis=-1)
    qm_ref[...] = (q2[:, :hw] * cos_m4 + q2[:, hw:] * sin_m4).astype(BF16)

    ckv = proj(G_CKV)
    ckvn = (ckv * lax.rsqrt(jnp.mean(ckv * ckv, axis=-1, keepdims=True) + EPS)
            * kvg_ref[...]).astype(BF16)
    kr = proj(G_KR) * cos_m + proj(G_KRSW) * sin_m
    kn = _dot(ckvn, wkn_ref[...])
    km_ref[...] = (kn + jnp.concatenate([kr] * MLA_HEADS, axis=-1)).astype(BF16)
    vmt_ref[0] = _dot_nt(wvt_ref[...], ckvn).astype(BF16)

    z = _gelu(proj(G_SGU))
    u = z[:, :SGU_WIDTH]
    v = _ln(z[:, SGU_WIDTH:]) * sg_ref[...] + sb_ref[...]
    vb = v.astype(BF16)
    tm = x.shape[0]
    for c in range(tm // SGU_CHUNK):
        r0 = c * SGU_CHUNK
        cols = []
        for hh in range(SGU_HEADS):
            c0 = hh * SGU_HEAD_DIM
            cols.append(_dot(sw_ref[hh], vb[r0:r0 + SGU_CHUNK, c0:c0 + SGU_HEAD_DIM]))
        mixed = jnp.concatenate(cols, axis=-1) + sbias_ref[...]
        ysgu_ref[r0:r0 + SGU_CHUNK, :] = (u[r0:r0 + SGU_CHUNK, :] * mixed).astype(BF16)


def _inproj(xs, mods, tabs, lw, *, n_lat_tiles, tiles_per_seq):
    n_all, D = xs.shape
    tm = ROW_TILE
    nt = n_all // tm
    n_batch = n_lat_tiles // tiles_per_seq

    def mod_map(i):
        return (jnp.where(i < n_lat_tiles, i // tiles_per_seq, n_batch), 0, 0)

    def pos_map(i):
        return (jnp.where(i < n_lat_tiles, i % tiles_per_seq, tiles_per_seq), 0)

    row = lambda w: pl.BlockSpec((tm, w), lambda i: (i, 0))
    full = lambda a: pl.BlockSpec(a.shape, lambda i: (0,) * a.ndim)
    cos_q, sin_q, cos_m, sin_m = tabs
    consts = [lw["w_in"], lw["q_norm"], lw["w_uq"], lw["kv_norm"], lw["w_kn"], lw["w_vt"],
              lw["sgu_g"], lw["sgu_b"], lw["sgu_w"], lw["sgu_bias"]]
    hw = MLA_HEADS * HEAD_PAD
    out_shape = [
        jax.ShapeDtypeStruct((n_all, D_POOL), F32),
        jax.ShapeDtypeStruct((n_all, SWA_HEADS * HEAD_PAD), BF16),
        jax.ShapeDtypeStruct((n_all, SWA_KV_HEADS * HEAD_PAD), BF16),
        jax.ShapeDtypeStruct((n_all, SWA_KV_HEADS * SWA_HEAD_DIM), BF16),
        jax.ShapeDtypeStruct((n_all, hw), BF16),
        jax.ShapeDtypeStruct((n_all, hw), BF16),
        jax.ShapeDtypeStruct((nt, MLA_HEADS * MLA_V, tm), BF16),
        jax.ShapeDtypeStruct((n_all, SGU_WIDTH), BF16),
    ]
    out_specs = [row(D_POOL), row(SWA_HEADS * HEAD_PAD), row(SWA_KV_HEADS * HEAD_PAD),
                 row(SWA_KV_HEADS * SWA_HEAD_DIM), row(hw), row(hw),
                 pl.BlockSpec((1, MLA_HEADS * MLA_V, tm), lambda i: (i, 0, 0)),
                 row(SGU_WIDTH)]
    return pl.pallas_call(
        _inproj_kernel,
        grid=(nt,),
        in_specs=[row(D),
                  pl.BlockSpec((1, 8, D), mod_map),
                  pl.BlockSpec((tm, cos_q.shape[1]), pos_map),
                  pl.BlockSpec((tm, sin_q.shape[1]), pos_map),
                  pl.BlockSpec((tm, cos_m.shape[1]), pos_map),
                  pl.BlockSpec((tm, sin_m.shape[1]), pos_map)] + [full(a) for a in consts],
        out_specs=out_specs,
        out_shape=out_shape,
        compiler_params=pltpu.CompilerParams(
            dimension_semantics=("arbitrary",), vmem_limit_bytes=VMEM_LIMIT),
        name="inproj",
    )(xs, mods, cos_q, sin_q, cos_m, sin_m, *consts)


def _pool_from_scratch(xe_ref, rows, t0, seq_len, pw_ref, ps_ref):
    def win(d):
        return xe_ref[HALO + d:HALO + d + rows, :]

    x = win(0)
    t = t0 + lax.broadcasted_iota(jnp.int32, (rows, 1), 0)
    lane_group = lax.broadcasted_iota(jnp.int32, (1, D_POOL), 1) // POOL_GROUP
    s = x
    lo_d, hi_d = 0, 0
    mean = jnp.zeros_like(x)
    for gi, w in enumerate(POOL_WINDOWS):
        for d in range(-(w // 2), lo_d):
            s = s + win(d)
        for d in range(hi_d + 1, w // 2):
            s = s + win(d)
        lo_d, hi_d = -(w // 2), w // 2 - 1
        cnt = (jnp.minimum(t + w // 2, seq_len) - jnp.maximum(t - w // 2, 0)).astype(F32)
        mean = jnp.where(lane_group == gi, s / cnt, mean)
    d = (mean - x).astype(BF16)
    return _dot(d, pw_ref[...]) * ps_ref[...]


def _sink_softmax_pv(s_list, v_list, sink):
    m = sink
    for s in s_list:
        m = jnp.maximum(m, jnp.max(s, axis=-1, keepdims=True))
    denom = jnp.exp(sink - m)
    o = None
    for s, v in zip(s_list, v_list):
        p = jnp.exp(s - m)
        denom = denom + jnp.sum(p, axis=-1, keepdims=True)
        t = _dot(p.astype(BF16), v)
        o = t if o is None else o + t
    return o / denom


def _split_v(v):
    lane = lax.broadcasted_iota(jnp.int32, v.shape, 1)
    zero = jnp.zeros_like(v)
    lo0 = jnp.where(lane < SWA_HEAD_DIM, v, zero)
    hi1 = jnp.where(lane >= SWA_HEAD_DIM, v, zero)
    swapped = jnp.concatenate([v[:, SWA_HEAD_DIM:], v[:, :SWA_HEAD_DIM]], axis=-1)
    hi0 = jnp.where(lane >= SWA_HEAD_DIM, swapped, zero)
    lo1 = jnp.where(lane < SWA_HEAD_DIM, swapped, zero)
    return ((lo0, hi0), (lo1, hi1))


def _local_kernel(sink_ref, zp_ref, zpp_ref, zpn_ref, q_ref, k_ref, kp_ref, kn_ref,
                  v_ref, vp_ref, vn_ref, kc_ref, vc_ref, pw_ref, ps_ref,
                  ypool_ref, yswa_ref, xe_ref, ke_ref, ve_ref, *, seq_len):
    j = pl.program_id(1)
    rows = MIX_ROWS
    t0 = j * rows
    first = j == 0
    last = j == pl.num_programs(1) - 1

    xe_ref[0:HALO, :] = jnp.where(first, 0.0, zpp_ref[...])
    xe_ref[HALO:HALO + rows, :] = zp_ref[...]
    xe_ref[HALO + rows:, :] = jnp.where(last, 0.0, zpn_ref[...])
    ypool_ref[...] = _pool_from_scratch(xe_ref, rows, t0, seq_len, pw_ref, ps_ref).astype(BF16)

    ke_ref[0:BLOCK, :] = kp_ref[...]
    ke_ref[BLOCK:BLOCK + rows, :] = k_ref[...]
    ke_ref[BLOCK + rows:, :] = kn_ref[...]
    ve_ref[0:BLOCK, :] = vp_ref[...]
    ve_ref[BLOCK:BLOCK + rows, :] = v_ref[...]
    ve_ref[BLOCK + rows:, :] = vn_ref[...]

    qi = lax.broadcasted_iota(jnp.int32, (BLOCK, 3 * BLOCK), 0)
    kj = lax.broadcasted_iota(jnp.int32, (BLOCK, 3 * BLOCK), 1)
    rel = kj - BLOCK - qi
    in_window = jnp.abs(rel) <= SWA_WINDOW
    vc_split = _split_v(vc_ref[...])
    grp = SWA_HEADS // SWA_KV_HEADS
    nqb = rows // BLOCK
    for qb in range(nqb):
        r0 = qb * BLOCK
        valid = in_window
        if qb == 0:
            valid = valid & ((kj >= BLOCK) | jnp.logical_not(first))
        if qb == nqb - 1:
            valid = valid & ((kj < 2 * BLOCK) | jnp.logical_not(last))
        vband_split = _split_v(ve_ref[r0:r0 + 3 * BLOCK, :])
        for g in range(SWA_KV_HEADS):
            kband = ke_ref[r0:r0 + 3 * BLOCK, g * HEAD_PAD:(g + 1) * HEAD_PAD]
            kc = kc_ref[:, g * HEAD_PAD:(g + 1) * HEAD_PAD]
            o = None
            for e in range(grp):
                hq = g * grp + e
                q = q_ref[r0:r0 + BLOCK, hq * HEAD_PAD:(hq + 1) * HEAD_PAD]
                s_band = jnp.where(valid, _dot_nt(q, kband), -jnp.inf)
                s_ctx = _dot_nt(q, kc)
                t = _sink_softmax_pv([s_band, s_ctx], [vband_split[g][e], vc_split[g][e]],
                                     sink_ref[hq])
                o = t if o is None else o + t
            yswa_ref[r0:r0 + BLOCK, g * LANE:(g + 1) * LANE] = o.astype(BF16)


def _local_mixers(sink, zpool, qs, ks, vs, lw, *, n_batch, seq_len, ctx_len):
    rows = MIX_ROWS
    spb = seq_len // rows
    bps = rows // BLOCK
    hps = rows // HALO
    n_lat = n_batch * seq_len
    ctx_blk0 = n_lat // ctx_len

    main = lambda w: pl.BlockSpec((rows, w), lambda b, j, *_: (b * spb + j, 0))

    def prev_map(unit):
        per = rows // unit
        return lambda b, j, *_: (jnp.maximum((b * spb + j) * per - 1, 0), 0)

    def next_map(unit):
        per = rows // unit
        last_blk = n_lat // unit - 1
        return lambda b, j, *_: (jnp.minimum((b * spb + j + 1) * per, last_blk), 0)

    ctx = lambda w: pl.BlockSpec((ctx_len, w), lambda b, j, *_: (ctx_blk0 + b, 0))
    full = lambda a: pl.BlockSpec(a.shape, lambda b, j, *_: (0,) * a.ndim)
    kw = SWA_KV_HEADS * HEAD_PAD
    vw = SWA_KV_HEADS * SWA_HEAD_DIM
    grid_spec = pltpu.PrefetchScalarGridSpec(
        num_scalar_prefetch=1,
        grid=(n_batch, spb),
        in_specs=[
            main(D_POOL),
            pl.BlockSpec((HALO, D_POOL), prev_map(HALO)),
            pl.BlockSpec((HALO, D_POOL), next_map(HALO)),
            main(SWA_HEADS * HEAD_PAD),
            main(kw),
            pl.BlockSpec((BLOCK, kw), prev_map(BLOCK)),
            pl.BlockSpec((BLOCK, kw), next_map(BLOCK)),
            main(vw),
            pl.BlockSpec((BLOCK, vw), prev_map(BLOCK)),
            pl.BlockSpec((BLOCK, vw), next_map(BLOCK)),
            ctx(kw), ctx(vw),
            full(lw["pool_w"]), full(lw["pool_scale"]),
        ],
        out_specs=[main(D_POOL), main(SWA_HEADS * SWA_HEAD_DIM)],
        scratch_shapes=[
            pltpu.VMEM((rows + 2 * HALO, D_POOL), F32),
            pltpu.VMEM((rows + 2 * BLOCK, kw), BF16),
            pltpu.VMEM((rows + 2 * BLOCK, vw), BF16),
        ],
    )
    del bps, hps
    return pl.pallas_call(
        functools.partial(_local_kernel, seq_len=seq_len),
        grid_spec=grid_spec,
        out_shape=[jax.ShapeDtypeStruct((n_lat, D_POOL), BF16),
                   jax.ShapeDtypeStruct((n_lat, SWA_HEADS * SWA_HEAD_DIM), BF16)],
        compiler_params=pltpu.CompilerParams(
            dimension_semantics=("arbitrary", "arbitrary"), vmem_limit_bytes=VMEM_LIMIT),
        name="local_mixers",
    )(sink, zpool, zpool, zpool, qs, ks, ks, ks, vs, vs, vs, ks, vs,
      lw["pool_w"], lw["pool_scale"])


def _ctx_kernel(sink_ref, zp_ref, q_ref, k_ref, v_ref, qm_ref, km_ref, vmt_ref, pw_ref, ps_ref,
                ypool_ref, yswa_ref, ymla_ref, xe_ref, *, ctx_len):
    rows = ctx_len
    xe_ref[0:HALO, :] = jnp.zeros((HALO, D_POOL), F32)
    xe_ref[HALO:HALO + rows, :] = zp_ref[...]
    xe_ref[HALO + rows:, :] = jnp.zeros((HALO, D_POOL), F32)
    ypool_ref[...] = _pool_from_scratch(xe_ref, rows, 0, ctx_len, pw_ref, ps_ref).astype(BF16)

    v_split = _split_v(v_ref[...])
    grp = SWA_HEADS // SWA_KV_HEADS
    for g in range(SWA_KV_HEADS):
        k = k_ref[:, g * HEAD_PAD:(g + 1) * HEAD_PAD]
        o = None
        for e in range(grp):
            hq = g * grp + e
            q = q_ref[:, hq * HEAD_PAD:(hq + 1) * HEAD_PAD]
            t = _sink_softmax_pv([_dot_nt(q, k)], [v_split[g][e]], sink_ref[hq])
            o = t if o is None else o + t
        yswa_ref[:, g * LANE:(g + 1) * LANE] = o.astype(BF16)

    outs = []
    for hh in range(MLA_HEADS):
        q = qm_ref[:, hh * HEAD_PAD:(hh + 1) * HEAD_PAD]
        k = km_ref[:, hh * HEAD_PAD:(hh + 1) * HEAD_PAD]
        st = _dot_nt(k, q)
        m = jnp.max(st, axis=0, keepdims=True)
        p = jnp.exp(st - m)
        l = jnp.sum(p, axis=0, keepdims=True)
        ot = _dot(vmt_ref[0, hh * MLA_V:(hh + 1) * MLA_V, :], p.astype(BF16))
        outs.append(ot / l)
    ymla_ref[...] = jnp.concatenate(outs, axis=0).T.astype(BF16)


def _ctx_mixers(sink, zpool, qs, ks, vs, qm, km, vmt, lw, *, n_batch, seq_len, ctx_len):
    n_lat = n_batch * seq_len
    blk0 = n_lat // ctx_len
    ctx_tile = n_lat // ROW_TILE
    cpt = ROW_TILE // ctx_len
    rowb = lambda w: pl.BlockSpec((ctx_len, w), lambda b, *_: (blk0 + b, 0))
    outb = lambda w: pl.BlockSpec((ctx_len, w), lambda b, *_: (b, 0))
    full = lambda a: pl.BlockSpec(a.shape, lambda b, *_: (0,) * a.ndim)
    hw = MLA_HEADS * HEAD_PAD
    grid_spec = pltpu.PrefetchScalarGridSpec(
        num_scalar_prefetch=1,
        grid=(n_batch,),
        in_specs=[rowb(D_POOL), rowb(SWA_HEADS * HEAD_PAD), rowb(SWA_KV_HEADS * HEAD_PAD),
                  rowb(SWA_KV_HEADS * SWA_HEAD_DIM), rowb(hw), rowb(hw),
                  pl.BlockSpec((1, MLA_HEADS * MLA_V, ctx_len),
                               lambda b, *_: (ctx_tile + b // cpt, 0, b % cpt)),
                  full(lw["pool_w"]), full(lw["pool_scale"])],
        out_specs=[outb(D_POOL), outb(SWA_HEADS * SWA_HEAD_DIM), outb(MLA_HEADS * MLA_V)],
        scratch_shapes=[pltpu.VMEM((ctx_len + 2 * HALO, D_POOL), F32)],
    )
    shp = jax.ShapeDtypeStruct((n_batch * ctx_len, 256), BF16)
    return pl.pallas_call(
        functools.partial(_ctx_kernel, ctx_len=ctx_len),
        grid_spec=grid_spec,
        out_shape=[shp, shp, shp],
        compiler_params=pltpu.CompilerParams(
            dimension_semantics=("arbitrary",), vmem_limit_bytes=VMEM_LIMIT),
        name="ctx_mixers",
    )(sink, zpool, qs, ks, vs, qm, km, vmt, lw["pool_w"], lw["pool_scale"])


def _mla_kernel(q_ref, k_ref, vt_ref, kc_ref, vtc_ref, o_ref, qt_ref, m_ref, l_ref, acc_ref):
    ki = pl.program_id(2)

    def step(h, k, vt, first):
        st = _dot(k, qt_ref[h * HEAD_PAD:(h + 1) * HEAD_PAD, :])
        m_tile = jnp.max(st, axis=0, keepdims=True)
        if first:
            m_new = m_tile
        else:
            m_old = m_ref[h:h + 1, :]
            m_new = jnp.maximum(m_old, m_tile)
            alpha = jnp.exp(m_old - m_new)
        p = jnp.exp(st - m_new)
        l_tile = jnp.sum(p, axis=0, keepdims=True)
        pv = _dot(vt, p.astype(BF16))
        rows = slice(h * MLA_V, (h + 1) * MLA_V)
        if first:
            l_ref[h:h + 1, :] = l_tile
            acc_ref[rows, :] = pv
        else:
            l_ref[h:h + 1, :] = alpha * l_ref[h:h + 1, :] + l_tile
            acc_ref[rows, :] = alpha * acc_ref[rows, :] + pv
        m_ref[h:h + 1, :] = m_new

    @pl.when(ki == 0)
    def _():
        qt_ref[...] = q_ref[...].T
        for h in range(MLA_HEADS):
            step(h, kc_ref[:, h * HEAD_PAD:(h + 1) * HEAD_PAD],
                 vtc_ref[0, h * MLA_V:(h + 1) * MLA_V, :], True)

    for h in range(MLA_HEADS):
        step(h, k_ref[:, h * HEAD_PAD:(h + 1) * HEAD_PAD],
             vt_ref[0, h * MLA_V:(h + 1) * MLA_V, :], False)

    @pl.when(ki == pl.num_programs(2) - 1)
    def _():
        parts = []
        for h in range(MLA_HEADS):
            parts.append(acc_ref[h * MLA_V:(h + 1) * MLA_V, :] / l_ref[h:h + 1, :])
        o_ref[...] = jnp.concatenate(parts, axis=0).T.astype(BF16)


def _mla_latent(qm, km, vmt, *, n_batch, seq_len, ctx_len):
    tq, tk = MLA_TQ, MLA_TK
    nq = seq_len // tq
    nk = seq_len // tk
    n_lat = n_batch * seq_len
    hw = MLA_HEADS * HEAD_PAD
    vw = MLA_HEADS * MLA_V
    ctx_blk0 = n_lat // ctx_len
    ctx_tile = n_lat // ROW_TILE
    cpt = ROW_TILE // ctx_len
    return pl.pallas_call(
        _mla_kernel,
        grid=(n_batch, nq, nk),
        in_specs=[
            pl.BlockSpec((tq, hw), lambda b, i, j: (b * nq + i, 0)),
            pl.BlockSpec((tk, hw), lambda b, i, j: (b * nk + j, 0)),
            pl.BlockSpec((1, vw, tk), lambda b, i, j: (b * nk + j, 0, 0)),
            pl.BlockSpec((ctx_len, hw), lambda b, i, j: (ctx_blk0 + b, 0)),
            pl.BlockSpec((1, vw, ctx_len), lambda b, i, j: (ctx_tile + b // cpt, 0, b % cpt)),
        ],
        out_specs=pl.BlockSpec((tq, vw), lambda b, i, j: (b * nq + i, 0)),
        out_shape=jax.ShapeDtypeStruct((n_lat, vw), BF16),
        scratch_shapes=[pltpu.VMEM((hw, tq), BF16),
                        pltpu.VMEM((8, tq), F32),
                        pltpu.VMEM((8, tq), F32),
                        pltpu.VMEM((vw, tq), F32)],
        compiler_params=pltpu.CompilerParams(
            dimension_semantics=("arbitrary", "arbitrary", "arbitrary"),
            vmem_limit_bytes=VMEM_LIMIT),
        name="mla_latent",
    )(qm, km, vmt, km, vmt)


def _outffn_kernel(x_ref, yp_ref, ys_ref, ym_ref, yg_ref, mod_ref, wout_ref, g1_ref, b1_ref,
                   w1_ref, w2_ref, g2_ref, b2_ref, o_ref, *, alpha):
    x = x_ref[...]
    q = wout_ref.shape[0] // 4
    y = (_dot(yp_ref[...], wout_ref[0:q, :]) + _dot(ys_ref[...], wout_ref[q:2 * q, :])
         + _dot(ym_ref[...], wout_ref[2 * q:3 * q, :]) + _dot(yg_ref[...], wout_ref[3 * q:, :]))
    gate1 = mod_ref[0, 2:3, :]
    shift = mod_ref[0, 3:4, :]
    scale = mod_ref[0, 4:5, :]
    gate2 = mod_ref[0, 5:6, :]
    x1 = _ln(alpha * x + gate1 * y) * g1_ref[...] + b1_ref[...]
    h = (_ln(x1) * (1.0 + scale) + shift).astype(BF16)
    dff = w1_ref.shape[1]
    f = None
    for c in range(dff // FF_CHUNK):
        a = jnp.maximum(_dot(h, w1_ref[:, c * FF_CHUNK:(c + 1) * FF_CHUNK]), 0.0)
        t = _dot((a * a).astype(BF16), w2_ref[c * FF_CHUNK:(c + 1) * FF_CHUNK, :])
        f = t if f is None else f + t
    o_ref[...] = _ln(alpha * x1 + gate2 * f) * g2_ref[...] + b2_ref[...]


def _outffn(xs, ys, mods, lw, *, n_tiles, n_lat_tiles, tiles_per_seq, alpha):
    n_all, D = xs.shape
    tm = ROW_TILE
    n_batch = n_lat_tiles // tiles_per_seq

    def mod_map(i):
        return (jnp.where(i < n_lat_tiles, i // tiles_per_seq, n_batch), 0, 0)

    row = lambda w: pl.BlockSpec((tm, w), lambda i: (i, 0))
    const = lambda a: pl.BlockSpec(a.shape, lambda i: (0,) * a.ndim,
                                   pipeline_mode=pl.Buffered(1))
    consts1 = [lw["w_out"], lw["ln1_g"], lw["ln1_b"], lw["w_ff1"], lw["w_ff2"],
               lw["ln2_g"], lw["ln2_b"]]
    return pl.pallas_call(
        functools.partial(_outffn_kernel, alpha=alpha),
        grid=(n_tiles,),
        in_specs=[row(D)] + [row(256)] * 4 + [pl.BlockSpec((1, 8, D), mod_map)]
                 + [const(a) for a in consts1],
        out_specs=row(D),
        out_shape=jax.ShapeDtypeStruct((n_tiles * tm, D), F32),
        compiler_params=pltpu.CompilerParams(
            dimension_semantics=("arbitrary",), vmem_limit_bytes=VMEM_LIMIT),
        name="outproj_ffn",
    )(xs, *ys, mods, *consts1)


def _swap_halves(w, n_axes_groups, half):
    shp = w.shape
    w = w.reshape(shp[:-1] + (n_axes_groups, 2, half))
    w = jnp.stack([-w[..., 1, :], w[..., 0, :]], axis=-2)
    return w.reshape(shp)


def _pad_heads(w, heads, dim):
    shp = w.shape
    w = w.reshape(shp[:-1] + (heads, dim))
    w = jnp.pad(w, [(0, 0)] * (w.ndim - 1) + [(0, HEAD_PAD - dim)])
    return w.reshape(shp[:-1] + (heads * HEAD_PAD,))


def _prepare_weights(w_in, pool_w, pool_scale, mla_q_norm, mla_w_uq, mla_kv_norm, mla_w_ukv,
                     sgu_norm_g, sgu_norm_b, sgu_w, sgu_b, w_out, ln1_g, ln1_b, w_ff1, w_ff2,
                     ln2_g, ln2_b):
    L, D, _ = w_in.shape
    c = 0
    w_pool = w_in[..., c:c + 256]; c += 256
    w_q = w_in[..., c:c + 256] * (SWA_HEAD_DIM ** -0.5); c += 256
    w_cq = w_in[..., c:c + 256]; c += 256
    w_sgu = w_in[..., c:c + 512]; c += 512
    w_k = w_in[..., c:c + 128]; c += 128
    w_v = w_in[..., c:c + 128]; c += 128
    w_ckv = w_in[..., c:c + 128]; c += 128
    w_kr = w_in[..., c:c + 32]
    rot = SWA_HEAD_DIM // 4
    w_qsw = _swap_halves(w_q, SWA_HEADS * 2, rot)
    w_ksw = _swap_halves(w_k, SWA_KV_HEADS * 2, rot)
    w_krsw = _swap_halves(w_kr, 2, MLA_ROPE // 4)

    def kr_pad(w):
        return jnp.pad(w, [(0, 0), (0, 0), (MLA_NOPE, HEAD_PAD - MLA_NOPE - MLA_ROPE)])

    w_in_ext = jnp.concatenate([
        w_pool, _pad_heads(w_q, SWA_HEADS, SWA_HEAD_DIM), _pad_heads(w_qsw, SWA_HEADS, SWA_HEAD_DIM),
        w_cq, w_sgu, _pad_heads(w_k, SWA_KV_HEADS, SWA_HEAD_DIM),
        _pad_heads(w_ksw, SWA_KV_HEADS, SWA_HEAD_DIM), w_v, w_ckv, kr_pad(w_kr), kr_pad(w_krsw),
    ], axis=-1).astype(BF16)
    assert w_in_ext.shape[-1] == D_IN_EXT

    qd = MLA_NOPE + MLA_ROPE
    wq = mla_w_uq.reshape(L, MLA_Q_RANK, MLA_HEADS, qd) * (qd ** -0.5)
    wq_rope_sw = _swap_halves(wq[..., MLA_NOPE:], 2, MLA_ROPE // 4)
    pad_q = [(0, 0), (0, 0), (0, 0), (0, HEAD_PAD - qd)]
    wq_full = jnp.pad(wq, pad_q).reshape(L, MLA_Q_RANK, MLA_HEADS * HEAD_PAD)
    wq_sw = jnp.pad(wq_rope_sw, [(0, 0), (0, 0), (0, 0), (MLA_NOPE, HEAD_PAD - qd)])
    wq_sw = wq_sw.reshape(L, MLA_Q_RANK, MLA_HEADS * HEAD_PAD)
    w_uq_ext = jnp.concatenate([wq_full, wq_sw], axis=-1).astype(BF16)

    wkv = mla_w_ukv.reshape(L, MLA_KV_RANK, MLA_HEADS, MLA_NOPE + MLA_V)
    w_kn = jnp.pad(wkv[..., :MLA_NOPE], [(0, 0), (0, 0), (0, 0), (0, HEAD_PAD - MLA_NOPE)])
    w_kn = w_kn.reshape(L, MLA_KV_RANK, MLA_HEADS * HEAD_PAD).astype(BF16)
    w_vt = wkv[..., MLA_NOPE:].reshape(L, MLA_KV_RANK, MLA_HEADS * MLA_V)
    w_vt = jnp.swapaxes(w_vt, 1, 2).astype(BF16)

    g_n = len(POOL_WINDOWS)
    eye = jnp.eye(g_n, dtype=F32)
    pool_bd = (pool_w[:, :, :, None, :] * eye[None, :, None, :, None])
    pool_bd = pool_bd.reshape(L, g_n * POOL_GROUP, g_n * POOL_GROUP).astype(BF16)

    sgu_bias = jnp.repeat(jnp.swapaxes(sgu_b, 1, 2), SGU_HEAD_DIM, axis=-1)

    r2 = lambda a: a.reshape(L, 1, a.shape[-1])
    return dict(
        w_in=w_in_ext, q_norm=r2(mla_q_norm), w_uq=w_uq_ext, kv_norm=r2(mla_kv_norm),
        w_kn=w_kn, w_vt=w_vt, sgu_g=r2(sgu_norm_g), sgu_b=r2(sgu_norm_b),
        sgu_w=sgu_w.astype(BF16), sgu_bias=sgu_bias,
        pool_w=pool_bd, pool_scale=r2(pool_scale),
        w_out=w_out.astype(BF16), ln1_g=r2(ln1_g), ln1_b=r2(ln1_b),
        w_ff1=w_ff1.astype(BF16), w_ff2=w_ff2.astype(BF16), ln2_g=r2(ln2_g), ln2_b=r2(ln2_b),
    )


def _rope_tables(n):
    pos = np.arange(n)
    row = (pos // GRID_W).astype(np.float32)
    col = (pos % GRID_W).astype(np.float32)

    def head_table(d_rot, lead, width):
        d_ax = d_rot // 2
        inv = ROPE_BASE ** (-jnp.arange(0, d_ax, 2, dtype=F32) / d_ax)
        ar = jnp.asarray(row)[:, None] * inv
        ac = jnp.asarray(col)[:, None] * inv
        cos = jnp.concatenate([jnp.cos(ar), jnp.cos(ar), jnp.cos(ac), jnp.cos(ac)], axis=-1)
        sin = jnp.concatenate([jnp.sin(ar), jnp.sin(ar), jnp.sin(ac), jnp.sin(ac)], axis=-1)
        tail = width - lead - d_rot
        cos = jnp.pad(cos, [(0, 0), (lead, tail)], constant_values=1.0)
        sin = jnp.pad(sin, [(0, 0), (lead, tail)])
        ident = (jnp.ones((ROW_TILE, width), F32), jnp.zeros((ROW_TILE, width), F32))
        return jnp.concatenate([cos, ident[0]], axis=0), jnp.concatenate([sin, ident[1]], axis=0)

    cos_h, sin_h = head_table(SWA_HEAD_DIM, 0, HEAD_PAD)
    cos_q = jnp.concatenate([cos_h] * SWA_HEADS, axis=-1)
    sin_q = jnp.concatenate([sin_h] * SWA_HEADS, axis=-1)
    cos_m, sin_m = head_table(MLA_ROPE, MLA_NOPE, HEAD_PAD)
    return cos_q, sin_q, cos_m, sin_m


def kernel(x, c, ctx, c_ctx, w_ada, b_ada, w_in, pool_w, pool_scale, swa_sink, mla_q_norm,
           mla_w_uq, mla_kv_norm, mla_w_ukv, sgu_norm_g, sgu_norm_b, sgu_w, sgu_b, w_out,
           ln1_g, ln1_b, w_ff1, w_ff2, ln2_g, ln2_b):
    B, n, D = x.shape
    C = ctx.shape[1]
    L = w_in.shape[0]
    assert n % ROW_TILE == 0 and n % MIX_ROWS == 0 and n % MLA_TQ == 0 and MLA_TK == ROW_TILE
    assert (B * C) % ROW_TILE == 0 and ROW_TILE % C == 0 and C % BLOCK == 0 and B + 1 <= 8
    alpha = (2 * L) ** 0.25
    n_lat = B * n
    n_lat_tiles = n_lat // ROW_TILE
    n_all_tiles = (n_lat + B * C) // ROW_TILE
    tiles_per_seq = n // ROW_TILE

    cond = jnp.concatenate([c, c_ctx[None, :], jnp.zeros((8 - B - 1, D), F32)], axis=0)
    mods_all = _modulation(cond, w_ada, b_ada)
    mods_all = jnp.pad(mods_all.reshape(L, 8, N_MOD, D),
                       [(0, 0), (0, 0), (0, 8 - N_MOD), (0, 0)])

    W = _prepare_weights(w_in, pool_w, pool_scale, mla_q_norm, mla_w_uq, mla_kv_norm, mla_w_ukv,
                         sgu_norm_g, sgu_norm_b, sgu_w, sgu_b, w_out, ln1_g, ln1_b, w_ff1,
                         w_ff2, ln2_g, ln2_b)
    tabs = _rope_tables(n)

    xs = jnp.concatenate([x.reshape(n_lat, D), ctx.reshape(B * C, D)], axis=0)
    for l in range(L):
        last = l == L - 1
        lw = {k: v[l] for k, v in W.items()}
        mods = mods_all[l]
        sink = swa_sink[l]
        zpool, qs, ks, vs, qm, km, vmt, ysgu = _inproj(
            xs, mods, tabs, lw, n_lat_tiles=n_lat_tiles, tiles_per_seq=tiles_per_seq)
        ypool, yswa = _local_mixers(sink, zpool, qs, ks, vs, lw,
                                    n_batch=B, seq_len=n, ctx_len=C)
        ymla = _mla_latent(qm, km, vmt, n_batch=B, seq_len=n, ctx_len=C)
        if last:
            ys = (ypool, yswa, ymla, ysgu)
            n_tiles = n_lat_tiles
        else:
            cp, cs, cm = _ctx_mixers(sink, zpool, qs, ks, vs, qm, km, vmt, lw,
                                     n_batch=B, seq_len=n, ctx_len=C)
            ys = (jnp.concatenate([ypool, cp], axis=0), jnp.concatenate([yswa, cs], axis=0),
                  jnp.concatenate([ymla, cm], axis=0), ysgu)
            n_tiles = n_all_tiles
        xs = _outffn(xs, ys, mods, lw, n_tiles=n_tiles, n_lat_tiles=n_lat_tiles,
                     tiles_per_seq=tiles_per_seq, alpha=alpha)
    return xs[:n_lat].reshape(B, n, D)
```

```python
import functools
import math

import jax
import jax.numpy as jnp
import numpy as np
from jax import lax
from jax.experimental import pallas as pl
from jax.experimental.pallas import tpu as pltpu

F32 = jnp.float32
BF16 = jnp.bfloat16

GRID_W = 64
POOL_WINDOWS = (2, 4, 8, 16)
POOL_GROUP = 64
D_POOL = 256
SWA_HEADS = 4
SWA_KV_HEADS = 2
SWA_HEAD_DIM = 64
SWA_WINDOW = 128
BLOCK = 128
MLA_HEADS = 4
MLA_NOPE = 64
MLA_ROPE = 32
MLA_V = 64
MLA_Q_RANK = 256
MLA_KV_RANK = 128
SGU_HEADS = 4
SGU_WIDTH = 256
SGU_HEAD_DIM = 64
SGU_CHUNK = 128
ROPE_BASE = 10000.0
N_MOD = 6
EPS = 1e-6

LANE = 128
ROW_TILE = 512
MIX_ROWS = 512
HALO = 8
MLA_TQ = 512
MLA_QSUB = 256
MLA_KSUB = 256
DENOM_ROWS = 16
MLA_TILES_PER_ITER = 4
SOFTMAX_AHEAD = 8
LOG2E = math.log2(math.e)
FF_CHUNK = 1024
VMEM_LIMIT = 56 * 1024 * 1024

HEAD_PAD = LANE

G_POOL = (0, 256)
G_Q = (256, 512)
G_QSW = (768, 512)
G_CQ = (1280, 256)
G_SGU = (1536, 512)
G_K = (2048, 256)
G_KSW = (2304, 256)
G_V = (2560, 128)
G_CKV = (2688, 128)
G_KR = (2816, 128)
G_KRSW = (2944, 128)
D_IN_EXT = 3072


def _ln(x):
    mu = jnp.mean(x, axis=-1, keepdims=True)
    xc = x - mu
    var = jnp.mean(xc * xc, axis=-1, keepdims=True)
    return xc * lax.rsqrt(var + EPS)


def _dot(a, b):
    return jnp.dot(a, b, preferred_element_type=F32)


def _dot_nt(a, b):
    return lax.dot_general(a, b, (((1,), (1,)), ((), ())), preferred_element_type=F32)


def _gelu(x):
    return 0.5 * x * (1.0 + lax.erf(x * (1.0 / math.sqrt(2.0))))


def _mod_kernel(c_ref, w_ref, b_ref, o_ref):
    c = c_ref[...]
    s = c * (1.0 / (1.0 + jnp.exp(-c)))
    o_ref[0] = jnp.dot(s, w_ref[0], preferred_element_type=F32,
                       precision=lax.Precision.HIGHEST) + b_ref[0]


def _modulation(cond, w_ada, b_ada):
    L, D, D6 = w_ada.shape
    nc = D6 // D
    return pl.pallas_call(
        _mod_kernel,
        grid=(L, nc),
        in_specs=[
            pl.BlockSpec((8, D), lambda l, j: (0, 0)),
            pl.BlockSpec((1, D, D), lambda l, j: (l, 0, j)),
            pl.BlockSpec((1, 1, D), lambda l, j: (l, 0, j)),
        ],
        out_specs=pl.BlockSpec((1, 8, D), lambda l, j: (l, 0, j)),
        out_shape=jax.ShapeDtypeStruct((L, 8, D6), F32),
        compiler_params=pltpu.CompilerParams(
            dimension_semantics=("arbitrary", "arbitrary"), vmem_limit_bytes=VMEM_LIMIT),
        name="adaln_modulation",
    )(cond, w_ada, b_ada.reshape(L, 1, D6))


def _inproj_kernel(x_ref, mod_ref, cq_ref, sq_ref, cm_ref, sm_ref, win_ref,
                   qg_ref, wuq_ref, kvg_ref, wkn_ref, wvt_ref,
                   sg_ref, sb_ref, sw_ref, sbias_ref,
                   zpool_ref, qs_ref, ks_ref, vs_ref, vst_ref, qm_ref, km_ref, vmt_ref, ysgu_ref):
    x = x_ref[...]
    shift = mod_ref[0, 0:1, :]
    scale = mod_ref[0, 1:2, :]
    h = (_ln(x) * (1.0 + scale) + shift).astype(BF16)

    def proj(g):
        return _dot(h, win_ref[:, g[0]:g[0] + g[1]])

    zpool_ref[...] = proj(G_POOL)

    cos_q = cq_ref[...]
    sin_q = sq_ref[...]
    qs_ref[...] = (proj(G_Q) * cos_q + proj(G_QSW) * sin_q).astype(BF16)
    ks_ref[...] = (proj(G_K) * cos_q[:, :G_K[1]] + proj(G_KSW) * sin_q[:, :G_K[1]]).astype(BF16)
    v_swa = proj(G_V)
    vs_ref[...] = v_swa.astype(BF16)
    vst_ref[0] = v_swa.T.astype(BF16)

    cos_m = cm_ref[...]
    sin_m = sm_ref[...]
    cq = proj(G_CQ)
    cqn = cq * lax.rsqrt(jnp.mean(cq * cq, axis=-1, keepdims=True) + EPS) * qg_ref[...]
    q2 = _dot(cqn.astype(BF16), wuq_ref[...])
    hw = MLA_HEADS * HEAD_PAD
    cos_m4 = jnp.concatenate([cos_m] * MLA_HEADS, axis=-1)
    sin_m4 = jnp.concatenate([sin_m] * MLA_HEADS, axis=-1)
    qm_ref[...] = (q2[:, :hw] * cos_m4 + q2[:, hw:] * sin_m4).astype(BF16)

    ckv = proj(G_CKV)
    ckvn = (ckv * lax.rsqrt(jnp.mean(ckv * ckv, axis=-1, keepdims=True) + EPS)
            * kvg_ref[...]).astype(BF16)
    kr = proj(G_KR) * cos_m + proj(G_KRSW) * sin_m
    kn = _dot(ckvn, wkn_ref[...])
    km_ref[...] = (kn + jnp.concatenate([kr] * MLA_HEADS, axis=-1)).astype(BF16)
    vmt_ref[0] = _dot_nt(wvt_ref[...], ckvn).astype(BF16)

    z = _gelu(proj(G_SGU))
    u = z[:, :SGU_WIDTH]
    v = _ln(z[:, SGU_WIDTH:]) * sg_ref[...] + sb_ref[...]
    vb = v.astype(BF16)
    tm = x.shape[0]
    for c in range(tm // SGU_CHUNK):
        r0 = c * SGU_CHUNK
        cols = []
        for hh in range(SGU_HEADS):
            c0 = hh * SGU_HEAD_DIM
            cols.append(_dot(sw_ref[hh], vb[r0:r0 + SGU_CHUNK, c0:c0 + SGU_HEAD_DIM]))
        mixed = jnp.concatenate(cols, axis=-1) + sbias_ref[...]
        ysgu_ref[r0:r0 + SGU_CHUNK, :] = (u[r0:r0 + SGU_CHUNK, :] * mixed).astype(BF16)


def _inproj(xs, mods, tabs, lw, *, n_lat_tiles, tiles_per_seq):
    n_all, D = xs.shape
    tm = ROW_TILE
    nt = n_all // tm
    n_batch = n_lat_tiles // tiles_per_seq

    def mod_map(i):
        return (jnp.where(i < n_lat_tiles, i // tiles_per_seq, n_batch), 0, 0)

    def pos_map(i):
        return (jnp.where(i < n_lat_tiles, i % tiles_per_seq, tiles_per_seq), 0)

    row = lambda w: pl.BlockSpec((tm, w), lambda i: (i, 0))
    full = lambda a: pl.BlockSpec(a.shape, lambda i: (0,) * a.ndim)
    cos_q, sin_q, cos_m, sin_m = tabs
    consts = [lw["w_in"], lw["q_norm"], lw["w_uq"], lw["kv_norm"], lw["w_kn"], lw["w_vt"],
              lw["sgu_g"], lw["sgu_b"], lw["sgu_w"], lw["sgu_bias"]]
    hw = MLA_HEADS * HEAD_PAD
    out_shape = [
        jax.ShapeDtypeStruct((n_all, D_POOL), F32),
        jax.ShapeDtypeStruct((n_all, SWA_HEADS * HEAD_PAD), BF16),
        jax.ShapeDtypeStruct((n_all, SWA_KV_HEADS * HEAD_PAD), BF16),
        jax.ShapeDtypeStruct((n_all, SWA_KV_HEADS * SWA_HEAD_DIM), BF16),
        jax.ShapeDtypeStruct((nt, SWA_KV_HEADS * SWA_HEAD_DIM, tm), BF16),
        jax.ShapeDtypeStruct((n_all, hw), BF16),
        jax.ShapeDtypeStruct((n_all, hw), BF16),
        jax.ShapeDtypeStruct((nt, MLA_HEADS * MLA_V, tm), BF16),
        jax.ShapeDtypeStruct((n_all, SGU_WIDTH), BF16),
    ]
    out_specs = [row(D_POOL), row(SWA_HEADS * HEAD_PAD), row(SWA_KV_HEADS * HEAD_PAD),
                 row(SWA_KV_HEADS * SWA_HEAD_DIM),
                 pl.BlockSpec((1, SWA_KV_HEADS * SWA_HEAD_DIM, tm), lambda i: (i, 0, 0)),
                 row(hw), row(hw),
                 pl.BlockSpec((1, MLA_HEADS * MLA_V, tm), lambda i: (i, 0, 0)),
                 row(SGU_WIDTH)]
    return pl.pallas_call(
        _inproj_kernel,
        grid=(nt,),
        in_specs=[row(D),
                  pl.BlockSpec((1, 8, D), mod_map),
                  pl.BlockSpec((tm, cos_q.shape[1]), pos_map),
                  pl.BlockSpec((tm, sin_q.shape[1]), pos_map),
                  pl.BlockSpec((tm, cos_m.shape[1]), pos_map),
                  pl.BlockSpec((tm, sin_m.shape[1]), pos_map)] + [full(a) for a in consts],
        out_specs=out_specs,
        out_shape=out_shape,
        compiler_params=pltpu.CompilerParams(
            dimension_semantics=("arbitrary",), vmem_limit_bytes=VMEM_LIMIT),
        name="inproj",
    )(xs, mods, cos_q, sin_q, cos_m, sin_m, *consts)


def _pool_from_scratch(xe_ref, rows, t0, seq_len, pw_ref, ps_ref):
    def win(d):
        return xe_ref[HALO + d:HALO + d + rows, :]

    x = win(0)
    t = t0 + lax.broadcasted_iota(jnp.int32, (rows, 1), 0)
    lane_group = lax.broadcasted_iota(jnp.int32, (1, D_POOL), 1) // POOL_GROUP
    s = x
    lo_d, hi_d = 0, 0
    mean = jnp.zeros_like(x)
    for gi, w in enumerate(POOL_WINDOWS):
        for d in range(-(w // 2), lo_d):
            s = s + win(d)
        for d in range(hi_d + 1, w // 2):
            s = s + win(d)
        lo_d, hi_d = -(w // 2), w // 2 - 1
        cnt = (jnp.minimum(t + w // 2, seq_len) - jnp.maximum(t - w // 2, 0)).astype(F32)
        mean = jnp.where(lane_group == gi, s / cnt, mean)
    d = (mean - x).astype(BF16)
    return _dot(d, pw_ref[...]) * ps_ref[...]


def _sink_softmax_pv(s_list, v_list, sink2):
    m = sink2
    for s in s_list:
        m = jnp.maximum(m, jnp.max(s, axis=-1, keepdims=True))
    denom = jnp.exp2(sink2 - m)
    o = None
    for s, v in zip(s_list, v_list):
        p = jnp.exp2(s - m)
        denom = denom + jnp.sum(p, axis=-1, keepdims=True)
        t = _dot(p.astype(BF16), v)
        o = t if o is None else o + t
    return o / denom


def _sublane_allmax(x):
    for shift in (4, 2, 1):
        x = jnp.maximum(x, pltpu.roll(x, shift, axis=0))
    return x


def _softmax_scores(k_sub, qt):
    return _dot(k_sub, qt).reshape(k_sub.shape[0] // 8, 8, qt.shape[1])


def _softmax_unit(s3, vt_sub, m, acc):
    ks = s3.shape[0] * 8
    qs = s3.shape[2]
    nv = acc.shape[0]
    m_new = jnp.maximum(m, _sublane_allmax(jnp.max(s3, axis=0)))
    alpha = jnp.exp2(m - m_new)
    p = jnp.exp2(s3 - m_new[None]).reshape(ks, qs).astype(BF16)
    vt_ext = jnp.concatenate([vt_sub, jnp.ones((DENOM_ROWS, ks), BF16)], axis=0)
    acc = (acc.reshape(nv // 8, 8, qs) * alpha[None]).reshape(nv, qs) + _dot(vt_ext, p)
    return m_new, acc


def _softmax_pipeline(units, score_of, value_of, load, store, *, is_first, is_last):
    pending = [score_of(u) for u in units[:SOFTMAX_AHEAD]]
    state = None
    for i, u in enumerate(units):
        if i + SOFTMAX_AHEAD < len(units):
            pending.append(score_of(units[i + SOFTMAX_AHEAD]))
        if is_first(u):
            state = load(u)
        state = _softmax_unit(pending.pop(0), value_of(u), *state)
        if is_last(u):
            store(u, state)


def _split_v(v):
    lane = lax.broadcasted_iota(jnp.int32, v.shape, 1)
    zero = jnp.zeros_like(v)
    lo0 = jnp.where(lane < SWA_HEAD_DIM, v, zero)
    hi1 = jnp.where(lane >= SWA_HEAD_DIM, v, zero)
    swapped = jnp.concatenate([v[:, SWA_HEAD_DIM:], v[:, :SWA_HEAD_DIM]], axis=-1)
    hi0 = jnp.where(lane >= SWA_HEAD_DIM, swapped, zero)
    lo1 = jnp.where(lane < SWA_HEAD_DIM, swapped, zero)
    return ((lo0, hi0), (lo1, hi1))


def _local_kernel(sink_ref, zp_ref, zpp_ref, zpn_ref, q_ref, k_ref, kp_ref, kn_ref,
                  vt_ref, vtp_ref, vtn_ref, kc_ref, vtc_ref, pw_ref, ps_ref,
                  ypool_ref, yswa_ref, xe_ref, qt_ref, ke_ref, vte_ref, ot_ref, *, seq_len):
    j = pl.program_id(1)
    rows = MIX_ROWS
    t0 = j * rows
    first = j == 0
    last = j == pl.num_programs(1) - 1

    xe_ref[0:HALO, :] = jnp.where(first, 0.0, zpp_ref[...])
    xe_ref[HALO:HALO + rows, :] = zp_ref[...]
    xe_ref[HALO + rows:, :] = jnp.where(last, 0.0, zpn_ref[...])
    ypool_ref[...] = _pool_from_scratch(xe_ref, rows, t0, seq_len, pw_ref, ps_ref).astype(BF16)

    qt_ref[...] = q_ref[...].T
    ke_ref[0:BLOCK, :] = kp_ref[...]
    ke_ref[BLOCK:BLOCK + rows, :] = k_ref[...]
    ke_ref[BLOCK + rows:, :] = kn_ref[...]
    vte_ref[:, 0:BLOCK] = vtp_ref[0]
    vte_ref[:, BLOCK:BLOCK + rows] = vt_ref[0]
    vte_ref[:, BLOCK + rows:] = vtn_ref[0]

    grp = SWA_HEADS // SWA_KV_HEADS
    qw = grp * BLOCK
    shape3 = (BLOCK // 8, 8, qw)
    key_j = (lax.broadcasted_iota(jnp.int32, shape3, 0) * 8
             + lax.broadcasted_iota(jnp.int32, shape3, 1))
    qry_i = lax.broadcasted_iota(jnp.int32, shape3, 2) % BLOCK
    rel = key_j - qry_i
    lane = lax.broadcasted_iota(jnp.int32, (8, qw), 1)
    nqb = rows // BLOCK
    n_ctx_sub = kc_ref.shape[0] // BLOCK

    kinds = ["cur", "prev", "next"] + ["ctx%d" % c for c in range(n_ctx_sub)]
    units = [(qb, g, kind) for qb in range(nqb) for g in range(SWA_KV_HEADS) for kind in kinds]

    def q_tile(qb, g):
        parts = [qt_ref[(g * grp + e) * HEAD_PAD:(g * grp + e + 1) * HEAD_PAD,
                        qb * BLOCK:(qb + 1) * BLOCK] for e in range(grp)]
        return jnp.concatenate(parts, axis=1)

    def scores(u):
        qb, g, kind = u
        gcols = slice(g * HEAD_PAD, (g + 1) * HEAD_PAD)
        if kind.startswith("ctx"):
            c = int(kind[3:])
            return _softmax_scores(kc_ref[c * BLOCK:(c + 1) * BLOCK, gcols], q_tile(qb, g))
        off = {"prev": 0, "cur": 1, "next": 2}[kind]
        r0 = (qb + off) * BLOCK
        s3 = _softmax_scores(ke_ref[r0:r0 + BLOCK, gcols], q_tile(qb, g))
        if kind == "prev":
            ok = rel >= 0
            if qb == 0:
                ok = ok & jnp.logical_not(first)
            s3 = jnp.where(ok, s3, -jnp.inf)
        elif kind == "next":
            ok = rel <= 0
            if qb == nqb - 1:
                ok = ok & jnp.logical_not(last)
            s3 = jnp.where(ok, s3, -jnp.inf)
        return s3

    def values(u):
        qb, g, kind = u
        grows = slice(g * SWA_HEAD_DIM, (g + 1) * SWA_HEAD_DIM)
        if kind.startswith("ctx"):
            c = int(kind[3:])
            return vtc_ref[0, grows, c * BLOCK:(c + 1) * BLOCK]
        off = {"prev": 0, "cur": 1, "next": 2}[kind]
        return vte_ref[grows, (qb + off) * BLOCK:(qb + off + 1) * BLOCK]

    def load(u):
        return (jnp.full((8, qw), -jnp.inf, F32),
                jnp.zeros((SWA_HEAD_DIM + DENOM_ROWS, qw), F32))

    def store(u, state):
        qb, g, _ = u
        m, acc = state
        sink2 = jnp.where(lane < BLOCK, sink_ref[g * grp], sink_ref[g * grp + 1]) * LOG2E
        m_fin = jnp.maximum(m, sink2)
        a = jnp.exp2(m - m_fin)
        denom = acc[SWA_HEAD_DIM:SWA_HEAD_DIM + 1]
        l_tot = denom * a[0:1] + jnp.exp2(sink2 - m_fin)[0:1]
        o = acc[:SWA_HEAD_DIM] * (a[0:1] / l_tot)
        for e in range(grp):
            hq = g * grp + e
            ot_ref[hq * SWA_HEAD_DIM:(hq + 1) * SWA_HEAD_DIM, qb * BLOCK:(qb + 1) * BLOCK] = (
                o[:, e * BLOCK:(e + 1) * BLOCK])

    _softmax_pipeline(units, scores, values, load, store,
                      is_first=lambda u: u[2] == kinds[0], is_last=lambda u: u[2] == kinds[-1])
    yswa_ref[...] = ot_ref[...].T.astype(BF16)


def _local_mixers(sink, zpool, qs, ks, vst, lw, *, n_batch, seq_len, ctx_len):
    rows = MIX_ROWS
    spb = seq_len // rows
    n_lat = n_batch * seq_len
    ctx_blk0 = n_lat // ctx_len
    ctx_tile = n_lat // ROW_TILE
    cpt = ROW_TILE // ctx_len
    bpt = ROW_TILE // BLOCK

    main = lambda w: pl.BlockSpec((rows, w), lambda b, j, *_: (b * spb + j, 0))

    def prev_blk(unit):
        per = rows // unit
        return lambda b, j: jnp.maximum((b * spb + j) * per - 1, 0)

    def next_blk(unit):
        per = rows // unit
        last_blk = n_lat // unit - 1
        return lambda b, j: jnp.minimum((b * spb + j + 1) * per, last_blk)

    def row_halo(unit, w, blk):
        return pl.BlockSpec((unit, w), lambda b, j, *_: (blk(unit)(b, j), 0))

    def lane_halo(blk):
        f = blk(BLOCK)
        return pl.BlockSpec((1, vw, BLOCK), lambda b, j, *_: (f(b, j) // bpt, 0, f(b, j) % bpt))

    full = lambda a: pl.BlockSpec(a.shape, lambda b, j, *_: (0,) * a.ndim)
    kw = SWA_KV_HEADS * HEAD_PAD
    vw = SWA_KV_HEADS * SWA_HEAD_DIM
    qw = SWA_HEADS * HEAD_PAD
    grid_spec = pltpu.PrefetchScalarGridSpec(
        num_scalar_prefetch=1,
        grid=(n_batch, spb),
        in_specs=[
            main(D_POOL),
            row_halo(HALO, D_POOL, prev_blk),
            row_halo(HALO, D_POOL, next_blk),
            main(qw),
            main(kw),
            row_halo(BLOCK, kw, prev_blk),
            row_halo(BLOCK, kw, next_blk),
            pl.BlockSpec((1, vw, rows), lambda b, j, *_: (b * spb + j, 0, 0)),
            lane_halo(prev_blk),
            lane_halo(next_blk),
            pl.BlockSpec((ctx_len, kw), lambda b, j, *_: (ctx_blk0 + b, 0)),
            pl.BlockSpec((1, vw, ctx_len), lambda b, j, *_: (ctx_tile + b // cpt, 0, b % cpt)),
            full(lw["pool_w"]), full(lw["pool_scale"]),
        ],
        out_specs=[main(D_POOL), main(SWA_HEADS * SWA_HEAD_DIM)],
        scratch_shapes=[
            pltpu.VMEM((rows + 2 * HALO, D_POOL), F32),
            pltpu.VMEM((qw, rows), BF16),
            pltpu.VMEM((rows + 2 * BLOCK, kw), BF16),
            pltpu.VMEM((vw, rows + 2 * BLOCK), BF16),
            pltpu.VMEM((SWA_HEADS * SWA_HEAD_DIM, rows), F32),
        ],
    )
    return pl.pallas_call(
        functools.partial(_local_kernel, seq_len=seq_len),
        grid_spec=grid_spec,
        out_shape=[jax.ShapeDtypeStruct((n_lat, D_POOL), BF16),
                   jax.ShapeDtypeStruct((n_lat, SWA_HEADS * SWA_HEAD_DIM), BF16)],
        compiler_params=pltpu.CompilerParams(
            dimension_semantics=("arbitrary", "arbitrary"), vmem_limit_bytes=VMEM_LIMIT),
        name="local_mixers",
    )(sink, zpool, zpool, zpool, qs, ks, ks, ks, vst, vst, vst, ks, vst,
      lw["pool_w"], lw["pool_scale"])


def _ctx_kernel(sink_ref, zp_ref, q_ref, k_ref, v_ref, qm_ref, km_ref, vmt_ref, pw_ref, ps_ref,
                ypool_ref, yswa_ref, ymla_ref, xe_ref, *, ctx_len):
    rows = ctx_len
    xe_ref[0:HALO, :] = jnp.zeros((HALO, D_POOL), F32)
    xe_ref[HALO:HALO + rows, :] = zp_ref[...]
    xe_ref[HALO + rows:, :] = jnp.zeros((HALO, D_POOL), F32)
    ypool_ref[...] = _pool_from_scratch(xe_ref, rows, 0, ctx_len, pw_ref, ps_ref).astype(BF16)

    v_split = _split_v(v_ref[...])
    grp = SWA_HEADS // SWA_KV_HEADS
    for g in range(SWA_KV_HEADS):
        k = k_ref[:, g * HEAD_PAD:(g + 1) * HEAD_PAD]
        o = None
        for e in range(grp):
            hq = g * grp + e
            q = q_ref[:, hq * HEAD_PAD:(hq + 1) * HEAD_PAD]
            t = _sink_softmax_pv([_dot_nt(q, k)], [v_split[g][e]], sink_ref[hq] * LOG2E)
            o = t if o is None else o + t
        yswa_ref[:, g * LANE:(g + 1) * LANE] = o.astype(BF16)

    outs = []
    for hh in range(MLA_HEADS):
        q = qm_ref[:, hh * HEAD_PAD:(hh + 1) * HEAD_PAD]
        k = km_ref[:, hh * HEAD_PAD:(hh + 1) * HEAD_PAD]
        st = _dot_nt(k, q)
        m = jnp.max(st, axis=0, keepdims=True)
        p = jnp.exp2(st - m)
        l = jnp.sum(p, axis=0, keepdims=True)
        ot = _dot(vmt_ref[0, hh * MLA_V:(hh + 1) * MLA_V, :], p.astype(BF16))
        outs.append(ot / l)
    ymla_ref[...] = jnp.concatenate(outs, axis=0).T.astype(BF16)


def _ctx_mixers(sink, zpool, qs, ks, vs, qm, km, vmt, lw, *, n_batch, seq_len, ctx_len):
    n_lat = n_batch * seq_len
    blk0 = n_lat // ctx_len
    ctx_tile = n_lat // ROW_TILE
    cpt = ROW_TILE // ctx_len
    rowb = lambda w: pl.BlockSpec((ctx_len, w), lambda b, *_: (blk0 + b, 0))
    outb = lambda w: pl.BlockSpec((ctx_len, w), lambda b, *_: (b, 0))
    full = lambda a: pl.BlockSpec(a.shape, lambda b, *_: (0,) * a.ndim)
    hw = MLA_HEADS * HEAD_PAD
    grid_spec = pltpu.PrefetchScalarGridSpec(
        num_scalar_prefetch=1,
        grid=(n_batch,),
        in_specs=[rowb(D_POOL), rowb(SWA_HEADS * HEAD_PAD), rowb(SWA_KV_HEADS * HEAD_PAD),
                  rowb(SWA_KV_HEADS * SWA_HEAD_DIM), rowb(hw), rowb(hw),
                  pl.BlockSpec((1, MLA_HEADS * MLA_V, ctx_len),
                               lambda b, *_: (ctx_tile + b // cpt, 0, b % cpt)),
                  full(lw["pool_w"]), full(lw["pool_scale"])],
        out_specs=[outb(D_POOL), outb(SWA_HEADS * SWA_HEAD_DIM), outb(MLA_HEADS * MLA_V)],
        scratch_shapes=[pltpu.VMEM((ctx_len + 2 * HALO, D_POOL), F32)],
    )
    shp = jax.ShapeDtypeStruct((n_batch * ctx_len, 256), BF16)
    return pl.pallas_call(
        functools.partial(_ctx_kernel, ctx_len=ctx_len),
        grid_spec=grid_spec,
        out_shape=[shp, shp, shp],
        compiler_params=pltpu.CompilerParams(
            dimension_semantics=("arbitrary",), vmem_limit_bytes=VMEM_LIMIT),
        name="ctx_mixers",
    )(sink, zpool, qs, ks, vs, qm, km, vmt, lw["pool_w"], lw["pool_scale"])


def _mla_kernel(q_ref, k_ref, vt_ref, kc_ref, vtc_ref, o_ref, qt_ref, m_ref, acc_ref, ot_ref,
                *, n_chunks, ctx_len):
    tq = q_ref.shape[0]
    qt_ref[...] = q_ref[...].T
    m_ref[...] = jnp.full(m_ref.shape, -jnp.inf, F32)
    acc_ref[...] = jnp.zeros(acc_ref.shape, F32)

    hcols = lambda h: slice(h * HEAD_PAD, (h + 1) * HEAD_PAD)
    vrows = lambda h: slice(h * MLA_V, (h + 1) * MLA_V)
    ksub = lambda s: slice(s * MLA_KSUB, (s + 1) * MLA_KSUB)
    qcols = lambda qh: slice(qh * MLA_QSUB, (qh + 1) * MLA_QSUB)

    def run_chunk(k_of, vt_of, n_sub):
        units = [(qh, h, s) for qh in range(tq // MLA_QSUB) for h in range(MLA_HEADS)
                 for s in range(n_sub)]

        def scores(u):
            qh, h, s = u
            return _softmax_scores(k_of(h, s), qt_ref[hcols(h), qcols(qh)])

        def load(u):
            qh, h, _ = u
            return (m_ref[h, :, qcols(qh)], acc_ref[h, :, qcols(qh)])

        def store(u, state):
            qh, h, _ = u
            m_ref[h, :, qcols(qh)] = state[0]
            acc_ref[h, :, qcols(qh)] = state[1]

        _softmax_pipeline(units, scores, lambda u: vt_of(u[1], u[2]), load, store,
                          is_first=lambda u: u[2] == 0, is_last=lambda u: u[2] == n_sub - 1)

    run_chunk(lambda h, s: kc_ref[ksub(s), hcols(h)],
              lambda h, s: vtc_ref[0, vrows(h), ksub(s)],
              ctx_len // MLA_KSUB)

    spt = ROW_TILE // MLA_KSUB

    def body(c, carry):
        r0 = pl.multiple_of(c * (MLA_TILES_PER_ITER * ROW_TILE), ROW_TILE)
        run_chunk(lambda h, s: k_ref[pl.ds(r0 + s * MLA_KSUB, MLA_KSUB), hcols(h)],
                  lambda h, s: vt_ref[c * MLA_TILES_PER_ITER + s // spt, vrows(h), ksub(s % spt)],
                  MLA_TILES_PER_ITER * spt)
        return carry

    lax.fori_loop(0, n_chunks // MLA_TILES_PER_ITER, body, 0)

    for h in range(MLA_HEADS):
        ot_ref[vrows(h), :] = acc_ref[h, :MLA_V, :] / acc_ref[h, MLA_V:MLA_V + 1, :]
    o_ref[...] = ot_ref[...].T.astype(BF16)


def _mla_latent(qm, km, vmt, *, n_batch, seq_len, ctx_len):
    tq = MLA_TQ
    nq = seq_len // tq
    tps = seq_len // ROW_TILE
    n_lat = n_batch * seq_len
    hw = MLA_HEADS * HEAD_PAD
    vw = MLA_HEADS * MLA_V
    ctx_blk0 = n_lat // ctx_len
    ctx_tile = n_lat // ROW_TILE
    cpt = ROW_TILE // ctx_len
    once = pl.Buffered(1)
    return pl.pallas_call(
        functools.partial(_mla_kernel, n_chunks=tps, ctx_len=ctx_len),
        grid=(n_batch, nq),
        in_specs=[
            pl.BlockSpec((tq, hw), lambda b, i: (b * nq + i, 0)),
            pl.BlockSpec((seq_len, hw), lambda b, i: (b, 0), pipeline_mode=once),
            pl.BlockSpec((tps, vw, ROW_TILE), lambda b, i: (b, 0, 0), pipeline_mode=once),
            pl.BlockSpec((ctx_len, hw), lambda b, i: (ctx_blk0 + b, 0)),
            pl.BlockSpec((1, vw, ctx_len), lambda b, i: (ctx_tile + b // cpt, 0, b % cpt)),
        ],
        out_specs=pl.BlockSpec((tq, vw), lambda b, i: (b * nq + i, 0)),
        out_shape=jax.ShapeDtypeStruct((n_lat, vw), BF16),
        scratch_shapes=[pltpu.VMEM((hw, tq), BF16),
                        pltpu.VMEM((MLA_HEADS, 8, tq), F32),
                        pltpu.VMEM((MLA_HEADS, MLA_V + DENOM_ROWS, tq), F32),
                        pltpu.VMEM((vw, tq), F32)],
        compiler_params=pltpu.CompilerParams(
            dimension_semantics=("arbitrary", "arbitrary"), vmem_limit_bytes=VMEM_LIMIT),
        name="mla_latent",
    )(qm, km, vmt, km, vmt)


def _outffn_kernel(x_ref, yp_ref, ys_ref, ym_ref, yg_ref, mod_ref, wout_ref, g1_ref, b1_ref,
                   w1_ref, w2_ref, g2_ref, b2_ref, o_ref, *, alpha):
    x = x_ref[...]
    q = wout_ref.shape[0] // 4
    y = (_dot(yp_ref[...], wout_ref[0:q, :]) + _dot(ys_ref[...], wout_ref[q:2 * q, :])
         + _dot(ym_ref[...], wout_ref[2 * q:3 * q, :]) + _dot(yg_ref[...], wout_ref[3 * q:, :]))
    gate1 = mod_ref[0, 2:3, :]
    shift = mod_ref[0, 3:4, :]
    scale = mod_ref[0, 4:5, :]
    gate2 = mod_ref[0, 5:6, :]
    x1 = _ln(alpha * x + gate1 * y) * g1_ref[...] + b1_ref[...]
    h = (_ln(x1) * (1.0 + scale) + shift).astype(BF16)
    dff = w1_ref.shape[1]
    f = None
    for c in range(dff // FF_CHUNK):
        a = jnp.maximum(_dot(h, w1_ref[:, c * FF_CHUNK:(c + 1) * FF_CHUNK]), 0.0)
        t = _dot((a * a).astype(BF16), w2_ref[c * FF_CHUNK:(c + 1) * FF_CHUNK, :])
        f = t if f is None else f + t
    o_ref[...] = _ln(alpha * x1 + gate2 * f) * g2_ref[...] + b2_ref[...]


def _outffn(xs, ys, mods, lw, *, n_tiles, n_lat_tiles, tiles_per_seq, alpha):
    n_all, D = xs.shape
    tm = ROW_TILE
    n_batch = n_lat_tiles // tiles_per_seq

    def mod_map(i):
        return (jnp.where(i < n_lat_tiles, i // tiles_per_seq, n_batch), 0, 0)

    row = lambda w: pl.BlockSpec((tm, w), lambda i: (i, 0))
    const = lambda a: pl.BlockSpec(a.shape, lambda i: (0,) * a.ndim,
                                   pipeline_mode=pl.Buffered(1))
    consts1 = [lw["w_out"], lw["ln1_g"], lw["ln1_b"], lw["w_ff1"], lw["w_ff2"],
               lw["ln2_g"], lw["ln2_b"]]
    return pl.pallas_call(
        functools.partial(_outffn_kernel, alpha=alpha),
        grid=(n_tiles,),
        in_specs=[row(D)] + [row(256)] * 4 + [pl.BlockSpec((1, 8, D), mod_map)]
                 + [const(a) for a in consts1],
        out_specs=row(D),
        out_shape=jax.ShapeDtypeStruct((n_tiles * tm, D), F32),
        compiler_params=pltpu.CompilerParams(
            dimension_semantics=("arbitrary",), vmem_limit_bytes=VMEM_LIMIT),
        name="outproj_ffn",
    )(xs, *ys, mods, *consts1)


def _swap_halves(w, n_axes_groups, half):
    shp = w.shape
    w = w.reshape(shp[:-1] + (n_axes_groups, 2, half))
    w = jnp.stack([-w[..., 1, :], w[..., 0, :]], axis=-2)
    return w.reshape(shp)


def _pad_heads(w, heads, dim):
    shp = w.shape
    w = w.reshape(shp[:-1] + (heads, dim))
    w = jnp.pad(w, [(0, 0)] * (w.ndim - 1) + [(0, HEAD_PAD - dim)])
    return w.reshape(shp[:-1] + (heads * HEAD_PAD,))


def _prepare_weights(w_in, pool_w, pool_scale, mla_q_norm, mla_w_uq, mla_kv_norm, mla_w_ukv,
                     sgu_norm_g, sgu_norm_b, sgu_w, sgu_b, w_out, ln1_g, ln1_b, w_ff1, w_ff2,
                     ln2_g, ln2_b):
    L, D, _ = w_in.shape
    c = 0
    w_pool = w_in[..., c:c + 256]; c += 256
    w_q = w_in[..., c:c + 256] * (SWA_HEAD_DIM ** -0.5 * LOG2E); c += 256
    w_cq = w_in[..., c:c + 256]; c += 256
    w_sgu = w_in[..., c:c + 512]; c += 512
    w_k = w_in[..., c:c + 128]; c += 128
    w_v = w_in[..., c:c + 128]; c += 128
    w_ckv = w_in[..., c:c + 128]; c += 128
    w_kr = w_in[..., c:c + 32]
    rot = SWA_HEAD_DIM // 4
    w_qsw = _swap_halves(w_q, SWA_HEADS * 2, rot)
    w_ksw = _swap_halves(w_k, SWA_KV_HEADS * 2, rot)
    w_krsw = _swap_halves(w_kr, 2, MLA_ROPE // 4)

    def kr_pad(w):
        return jnp.pad(w, [(0, 0), (0, 0), (MLA_NOPE, HEAD_PAD - MLA_NOPE - MLA_ROPE)])

    w_in_ext = jnp.concatenate([
        w_pool, _pad_heads(w_q, SWA_HEADS, SWA_HEAD_DIM), _pad_heads(w_qsw, SWA_HEADS, SWA_HEAD_DIM),
        w_cq, w_sgu, _pad_heads(w_k, SWA_KV_HEADS, SWA_HEAD_DIM),
        _pad_heads(w_ksw, SWA_KV_HEADS, SWA_HEAD_DIM), w_v, w_ckv, kr_pad(w_kr), kr_pad(w_krsw),
    ], axis=-1).astype(BF16)
    assert w_in_ext.shape[-1] == D_IN_EXT

    qd = MLA_NOPE + MLA_ROPE
    wq = mla_w_uq.reshape(L, MLA_Q_RANK, MLA_HEADS, qd) * (qd ** -0.5 * LOG2E)
    wq_rope_sw = _swap_halves(wq[..., MLA_NOPE:], 2, MLA_ROPE // 4)
    pad_q = [(0, 0), (0, 0), (0, 0), (0, HEAD_PAD - qd)]
    wq_full = jnp.pad(wq, pad_q).reshape(L, MLA_Q_RANK, MLA_HEADS * HEAD_PAD)
    wq_sw = jnp.pad(wq_rope_sw, [(0, 0), (0, 0), (0, 0), (MLA_NOPE, HEAD_PAD - qd)])
    wq_sw = wq_sw.reshape(L, MLA_Q_RANK, MLA_HEADS * HEAD_PAD)
    w_uq_ext = jnp.concatenate([wq_full, wq_sw], axis=-1).astype(BF16)

    wkv = mla_w_ukv.reshape(L, MLA_KV_RANK, MLA_HEADS, MLA_NOPE + MLA_V)
    w_kn = jnp.pad(wkv[..., :MLA_NOPE], [(0, 0), (0, 0), (0, 0), (0, HEAD_PAD - MLA_NOPE)])
    w_kn = w_kn.reshape(L, MLA_KV_RANK, MLA_HEADS * HEAD_PAD).astype(BF16)
    w_vt = wkv[..., MLA_NOPE:].reshape(L, MLA_KV_RANK, MLA_HEADS * MLA_V)
    w_vt = jnp.swapaxes(w_vt, 1, 2).astype(BF16)

    g_n = len(POOL_WINDOWS)
    eye = jnp.eye(g_n, dtype=F32)
    pool_bd = (pool_w[:, :, :, None, :] * eye[None, :, None, :, None])
    pool_bd = pool_bd.reshape(L, g_n * POOL_GROUP, g_n * POOL_GROUP).astype(BF16)

    sgu_bias = jnp.repeat(jnp.swapaxes(sgu_b, 1, 2), SGU_HEAD_DIM, axis=-1)

    r2 = lambda a: a.reshape(L, 1, a.shape[-1])
    return dict(
        w_in=w_in_ext, q_norm=r2(mla_q_norm), w_uq=w_uq_ext, kv_norm=r2(mla_kv_norm),
        w_kn=w_kn, w_vt=w_vt, sgu_g=r2(sgu_norm_g), sgu_b=r2(sgu_norm_b),
        sgu_w=sgu_w.astype(BF16), sgu_bias=sgu_bias,
        pool_w=pool_bd, pool_scale=r2(pool_scale),
        w_out=w_out.astype(BF16), ln1_g=r2(ln1_g), ln1_b=r2(ln1_b),
        w_ff1=w_ff1.astype(BF16), w_ff2=w_ff2.astype(BF16), ln2_g=r2(ln2_g), ln2_b=r2(ln2_b),
    )


def _rope_tables(n):
    pos = np.arange(n)
    row = (pos // GRID_W).astype(np.float32)
    col = (pos % GRID_W).astype(np.float32)

    def head_table(d_rot, lead, width):
        d_ax = d_rot // 2
        inv = ROPE_BASE ** (-jnp.arange(0, d_ax, 2, dtype=F32) / d_ax)
        ar = jnp.asarray(row)[:, None] * inv
        ac = jnp.asarray(col)[:, None] * inv
        cos = jnp.concatenate([jnp.cos(ar), jnp.cos(ar), jnp.cos(ac), jnp.cos(ac)], axis=-1)
        sin = jnp.concatenate([jnp.sin(ar), jnp.sin(ar), jnp.sin(ac), jnp.sin(ac)], axis=-1)
        tail = width - lead - d_rot
        cos = jnp.pad(cos, [(0, 0), (lead, tail)], constant_values=1.0)
        sin = jnp.pad(sin, [(0, 0), (lead, tail)])
        ident = (jnp.ones((ROW_TILE, width), F32), jnp.zeros((ROW_TILE, width), F32))
        return jnp.concatenate([cos, ident[0]], axis=0), jnp.concatenate([sin, ident[1]], axis=0)

    cos_h, sin_h = head_table(SWA_HEAD_DIM, 0, HEAD_PAD)
    cos_q = jnp.concatenate([cos_h] * SWA_HEADS, axis=-1)
    sin_q = jnp.concatenate([sin_h] * SWA_HEADS, axis=-1)
    cos_m, sin_m = head_table(MLA_ROPE, MLA_NOPE, HEAD_PAD)
    return cos_q, sin_q, cos_m, sin_m


def kernel(x, c, ctx, c_ctx, w_ada, b_ada, w_in, pool_w, pool_scale, swa_sink, mla_q_norm,
           mla_w_uq, mla_kv_norm, mla_w_ukv, sgu_norm_g, sgu_norm_b, sgu_w, sgu_b, w_out,
           ln1_g, ln1_b, w_ff1, w_ff2, ln2_g, ln2_b):
    B, n, D = x.shape
    C = ctx.shape[1]
    L = w_in.shape[0]
    assert n % (MLA_TILES_PER_ITER * ROW_TILE) == 0 and MIX_ROWS == ROW_TILE and n % MLA_TQ == 0
    assert MLA_TQ % MLA_QSUB == 0 and ROW_TILE % MLA_KSUB == 0 and C % MLA_KSUB == 0
    assert (B * C) % ROW_TILE == 0 and ROW_TILE % C == 0 and C % BLOCK == 0 and B + 1 <= 8
    alpha = (2 * L) ** 0.25
    n_lat = B * n
    n_lat_tiles = n_lat // ROW_TILE
    n_all_tiles = (n_lat + B * C) // ROW_TILE
    tiles_per_seq = n // ROW_TILE

    cond = jnp.concatenate([c, c_ctx[None, :], jnp.zeros((8 - B - 1, D), F32)], axis=0)
    mods_all = _modulation(cond, w_ada, b_ada)
    mods_all = jnp.pad(mods_all.reshape(L, 8, N_MOD, D),
                       [(0, 0), (0, 0), (0, 8 - N_MOD), (0, 0)])

    W = _prepare_weights(w_in, pool_w, pool_scale, mla_q_norm, mla_w_uq, mla_kv_norm, mla_w_ukv,
                         sgu_norm_g, sgu_norm_b, sgu_w, sgu_b, w_out, ln1_g, ln1_b, w_ff1,
                         w_ff2, ln2_g, ln2_b)
    tabs = _rope_tables(n)

    xs = jnp.concatenate([x.reshape(n_lat, D), ctx.reshape(B * C, D)], axis=0)
    for l in range(L):
        last = l == L - 1
        lw = {k: v[l] for k, v in W.items()}
        mods = mods_all[l]
        sink = swa_sink[l]
        zpool, qs, ks, vs, vst, qm, km, vmt, ysgu = _inproj(
            xs, mods, tabs, lw, n_lat_tiles=n_lat_tiles, tiles_per_seq=tiles_per_seq)
        ypool, yswa = _local_mixers(sink, zpool, qs, ks, vst, lw,
                                    n_batch=B, seq_len=n, ctx_len=C)
        ymla = _mla_latent(qm, km, vmt, n_batch=B, seq_len=n, ctx_len=C)
        if last:
            ys = (ypool, yswa, ymla, ysgu)
            n_tiles = n_lat_tiles
        else:
            cp, cs, cm = _ctx_mixers(sink, zpool, qs, ks, vs, qm, km, vmt, lw,
                                     n_batch=B, seq_len=n, ctx_len=C)
            ys = (jnp.concatenate([ypool, cp], axis=0), jnp.concatenate([yswa, cs], axis=0),
                  jnp.concatenate([ymla, cm], axis=0), ysgu)
            n_tiles = n_all_tiles
        xs = _outffn(xs, ys, mods, lw, n_tiles=n_tiles, n_lat_tiles=n_lat_tiles,
                     tiles_per_seq=tiles_per_seq, alpha=alpha)
    return xs[:n_lat].reshape(B, n, D)
```

```python
import functools
import math

import jax
import jax.numpy as jnp
import numpy as np
from jax import lax
from jax.experimental import pallas as pl
from jax.experimental.pallas import tpu as pltpu

F32 = jnp.float32
BF16 = jnp.bfloat16

GRID_W = 64
POOL_WINDOWS = (2, 4, 8, 16)
POOL_GROUP = 64
D_POOL = 256
SWA_HEADS = 4
SWA_KV_HEADS = 2
SWA_HEAD_DIM = 64
SWA_WINDOW = 128
BLOCK = 128
MLA_HEADS = 4
MLA_NOPE = 64
MLA_ROPE = 32
MLA_V = 64
MLA_Q_RANK = 256
MLA_KV_RANK = 128
SGU_HEADS = 4
SGU_WIDTH = 256
SGU_HEAD_DIM = 64
SGU_CHUNK = 128
ROPE_BASE = 10000.0
N_MOD = 6
EPS = 1e-6

LANE = 128
ROW_TILE = 512
MIX_ROWS = 512
HALO = 8
MLA_TQ = 1024
MLA_QSUB = 256
MLA_KSUB = 256
DENOM_ROWS = 16
MLA_TILES_PER_ITER = 4
SOFTMAX_AHEAD = 8
LOG2E = math.log2(math.e)
FF_CHUNK = 1024
FFN_ROW_GROUPS = 2
VMEM_LIMIT = 56 * 1024 * 1024

HEAD_PAD = LANE

G_POOL = (0, 256)
G_Q = (256, 512)
G_QSW = (768, 512)
G_CQ = (1280, 256)
G_SGU = (1536, 512)
G_K = (2048, 256)
G_KSW = (2304, 256)
G_V = (2560, 128)
G_CKV = (2688, 128)
G_KR = (2816, 128)
G_KRSW = (2944, 128)
D_IN_EXT = 3072


def _ln(x):
    mu = jnp.mean(x, axis=-1, keepdims=True)
    xc = x - mu
    var = jnp.mean(xc * xc, axis=-1, keepdims=True)
    return xc * lax.rsqrt(var + EPS)


def _dot(a, b):
    return jnp.dot(a, b, preferred_element_type=F32)


def _dot_nt(a, b):
    return lax.dot_general(a, b, (((1,), (1,)), ((), ())), preferred_element_type=F32)


def _gelu(x):
    return 0.5 * x * (1.0 + lax.erf(x * (1.0 / math.sqrt(2.0))))


def _mod_kernel(c_ref, w_ref, b_ref, o_ref):
    c = c_ref[...]
    s = c * (1.0 / (1.0 + jnp.exp(-c)))
    o_ref[0] = jnp.dot(s, w_ref[0], preferred_element_type=F32,
                       precision=lax.Precision.HIGHEST) + b_ref[0]


def _modulation(cond, w_ada, b_ada):
    L, D, D6 = w_ada.shape
    nc = D6 // D
    return pl.pallas_call(
        _mod_kernel,
        grid=(L, nc),
        in_specs=[
            pl.BlockSpec((8, D), lambda l, j: (0, 0)),
            pl.BlockSpec((1, D, D), lambda l, j: (l, 0, j)),
            pl.BlockSpec((1, 1, D), lambda l, j: (l, 0, j)),
        ],
        out_specs=pl.BlockSpec((1, 8, D), lambda l, j: (l, 0, j)),
        out_shape=jax.ShapeDtypeStruct((L, 8, D6), F32),
        compiler_params=pltpu.CompilerParams(
            dimension_semantics=("arbitrary", "arbitrary"), vmem_limit_bytes=VMEM_LIMIT),
        name="adaln_modulation",
    )(cond, w_ada, b_ada.reshape(L, 1, D6))


def _inproj_kernel(x_ref, mod_ref, cq_ref, sq_ref, cm_ref, sm_ref, win_ref,
                   qg_ref, wuq_ref, kvg_ref, wkn_ref, wvt_ref,
                   sg_ref, sb_ref, sw_ref, sbias_ref,
                   zpool_ref, qs_ref, ks_ref, vs_ref, vst_ref, qm_ref, km_ref, vmt_ref, ysgu_ref):
    x = x_ref[...]
    shift = mod_ref[0, 0:1, :]
    scale = mod_ref[0, 1:2, :]
    h = (_ln(x) * (1.0 + scale) + shift).astype(BF16)

    def proj(g):
        return _dot(h, win_ref[:, g[0]:g[0] + g[1]])

    zpool_ref[...] = proj(G_POOL)

    cos_h = cq_ref[...]
    sin_h = sq_ref[...]
    cos_q = jnp.concatenate([cos_h] * SWA_HEADS, axis=-1)
    sin_q = jnp.concatenate([sin_h] * SWA_HEADS, axis=-1)
    qs_ref[...] = (proj(G_Q) * cos_q + proj(G_QSW) * sin_q).astype(BF16)
    ks_ref[...] = (proj(G_K) * cos_q[:, :G_K[1]] + proj(G_KSW) * sin_q[:, :G_K[1]]).astype(BF16)
    v_swa = proj(G_V)
    vs_ref[...] = v_swa.astype(BF16)
    vst_ref[0] = v_swa.T.astype(BF16)

    cos_m = cm_ref[...]
    sin_m = sm_ref[...]
    cq = proj(G_CQ)
    cqn = cq * lax.rsqrt(jnp.mean(cq * cq, axis=-1, keepdims=True) + EPS) * qg_ref[...]
    q2 = _dot(cqn.astype(BF16), wuq_ref[...])
    hw = MLA_HEADS * HEAD_PAD
    cos_m4 = jnp.concatenate([cos_m] * MLA_HEADS, axis=-1)
    sin_m4 = jnp.concatenate([sin_m] * MLA_HEADS, axis=-1)
    qm_ref[...] = (q2[:, :hw] * cos_m4 + q2[:, hw:] * sin_m4).astype(BF16)

    ckv = proj(G_CKV)
    ckvn = (ckv * lax.rsqrt(jnp.mean(ckv * ckv, axis=-1, keepdims=True) + EPS)
            * kvg_ref[...]).astype(BF16)
    kr = proj(G_KR) * cos_m + proj(G_KRSW) * sin_m
    kn = _dot(ckvn, wkn_ref[...])
    km_ref[...] = (kn + jnp.concatenate([kr] * MLA_HEADS, axis=-1)).astype(BF16)
    vmt_ref[0] = _dot_nt(wvt_ref[...], ckvn).astype(BF16)

    z = _gelu(proj(G_SGU))
    u = z[:, :SGU_WIDTH]
    v = _ln(z[:, SGU_WIDTH:]) * sg_ref[...] + sb_ref[...]
    vb = v.astype(BF16)
    tm = x.shape[0]
    for c in range(tm // SGU_CHUNK):
        r0 = c * SGU_CHUNK
        cols = []
        for hh in range(SGU_HEADS):
            c0 = hh * SGU_HEAD_DIM
            cols.append(_dot(sw_ref[hh], vb[r0:r0 + SGU_CHUNK, c0:c0 + SGU_HEAD_DIM]))
        mixed = jnp.concatenate(cols, axis=-1) + sbias_ref[...]
        ysgu_ref[r0:r0 + SGU_CHUNK, :] = (u[r0:r0 + SGU_CHUNK, :] * mixed).astype(BF16)


def _inproj(xs, mods, tabs, lw, *, n_lat_tiles, tiles_per_seq):
    n_all, D = xs.shape
    tm = ROW_TILE
    nt = n_all // tm
    n_batch = n_lat_tiles // tiles_per_seq

    def mod_map(i):
        return (jnp.where(i < n_lat_tiles, i // tiles_per_seq, n_batch), 0, 0)

    def pos_map(i):
        return (jnp.where(i < n_lat_tiles, i % tiles_per_seq, tiles_per_seq), 0)

    row = lambda w: pl.BlockSpec((tm, w), lambda i: (i, 0))
    full = lambda a: pl.BlockSpec(a.shape, lambda i: (0,) * a.ndim)
    cos_q, sin_q, cos_m, sin_m = tabs
    consts = [lw["w_in"], lw["q_norm"], lw["w_uq"], lw["kv_norm"], lw["w_kn"], lw["w_vt"],
              lw["sgu_g"], lw["sgu_b"], lw["sgu_w"], lw["sgu_bias"]]
    hw = MLA_HEADS * HEAD_PAD
    out_shape = [
        jax.ShapeDtypeStruct((n_all, D_POOL), F32),
        jax.ShapeDtypeStruct((n_all, SWA_HEADS * HEAD_PAD), BF16),
        jax.ShapeDtypeStruct((n_all, SWA_KV_HEADS * HEAD_PAD), BF16),
        jax.ShapeDtypeStruct((n_all, SWA_KV_HEADS * SWA_HEAD_DIM), BF16),
        jax.ShapeDtypeStruct((nt, SWA_KV_HEADS * SWA_HEAD_DIM, tm), BF16),
        jax.ShapeDtypeStruct((n_all, hw), BF16),
        jax.ShapeDtypeStruct((n_all, hw), BF16),
        jax.ShapeDtypeStruct((nt, MLA_HEADS * MLA_V, tm), BF16),
        jax.ShapeDtypeStruct((n_all, SGU_WIDTH), BF16),
    ]
    out_specs = [row(D_POOL), row(SWA_HEADS * HEAD_PAD), row(SWA_KV_HEADS * HEAD_PAD),
                 row(SWA_KV_HEADS * SWA_HEAD_DIM),
                 pl.BlockSpec((1, SWA_KV_HEADS * SWA_HEAD_DIM, tm), lambda i: (i, 0, 0)),
                 row(hw), row(hw),
                 pl.BlockSpec((1, MLA_HEADS * MLA_V, tm), lambda i: (i, 0, 0)),
                 row(SGU_WIDTH)]
    return pl.pallas_call(
        _inproj_kernel,
        grid=(nt,),
        in_specs=[row(D),
                  pl.BlockSpec((1, 8, D), mod_map),
                  pl.BlockSpec((tm, cos_q.shape[1]), pos_map),
                  pl.BlockSpec((tm, sin_q.shape[1]), pos_map),
                  pl.BlockSpec((tm, cos_m.shape[1]), pos_map),
                  pl.BlockSpec((tm, sin_m.shape[1]), pos_map)] + [full(a) for a in consts],
        out_specs=out_specs,
        out_shape=out_shape,
        compiler_params=pltpu.CompilerParams(
            dimension_semantics=("arbitrary",), vmem_limit_bytes=VMEM_LIMIT),
        name="inproj",
    )(xs, mods, cos_q, sin_q, cos_m, sin_m, *consts)


def _pool_from_scratch(xe_ref, rows, t0, seq_len, pw_ref, ps_ref):
    def win(d):
        return xe_ref[HALO + d:HALO + d + rows, :]

    x = win(0)
    t = t0 + lax.broadcasted_iota(jnp.int32, (rows, 1), 0)
    lane_group = lax.broadcasted_iota(jnp.int32, (1, D_POOL), 1) // POOL_GROUP
    s = x
    lo_d, hi_d = 0, 0
    mean = jnp.zeros_like(x)
    for gi, w in enumerate(POOL_WINDOWS):
        for d in range(-(w // 2), lo_d):
            s = s + win(d)
        for d in range(hi_d + 1, w // 2):
            s = s + win(d)
        lo_d, hi_d = -(w // 2), w // 2 - 1
        cnt = (jnp.minimum(t + w // 2, seq_len) - jnp.maximum(t - w // 2, 0)).astype(F32)
        mean = jnp.where(lane_group == gi, s / cnt, mean)
    d = (mean - x).astype(BF16)
    return _dot(d, pw_ref[...]) * ps_ref[...]


def _sink_softmax_pv(s_list, v_list, sink2):
    m = sink2
    for s in s_list:
        m = jnp.maximum(m, jnp.max(s, axis=-1, keepdims=True))
    denom = jnp.exp2(sink2 - m)
    o = None
    for s, v in zip(s_list, v_list):
        p = jnp.exp2(s - m)
        denom = denom + jnp.sum(p, axis=-1, keepdims=True)
        t = _dot(p.astype(BF16), v)
        o = t if o is None else o + t
    return o / denom


def _sublane_allmax(x):
    for shift in (4, 2, 1):
        x = jnp.maximum(x, pltpu.roll(x, shift, axis=0))
    return x


def _softmax_scores(k_sub, qt):
    return _dot(k_sub, qt).reshape(k_sub.shape[0] // 8, 8, qt.shape[1])


def _softmax_unit(s3, vt_sub, m, acc):
    ks = s3.shape[0] * 8
    qs = s3.shape[2]
    nv = acc.shape[0]
    m_new = jnp.maximum(m, _sublane_allmax(jnp.max(s3, axis=0)))
    alpha = jnp.exp2(m - m_new)
    p = jnp.exp2(s3 - m_new[None]).reshape(ks, qs).astype(BF16)
    vt_ext = jnp.concatenate([vt_sub, jnp.ones((DENOM_ROWS, ks), BF16)], axis=0)
    acc = (acc.reshape(nv // 8, 8, qs) * alpha[None]).reshape(nv, qs) + _dot(vt_ext, p)
    return m_new, acc


def _softmax_pipeline(units, score_of, value_of, load, store, *, is_first, is_last):
    pending = [score_of(u) for u in units[:SOFTMAX_AHEAD]]
    state = None
    for i, u in enumerate(units):
        if i + SOFTMAX_AHEAD < len(units):
            pending.append(score_of(units[i + SOFTMAX_AHEAD]))
        if is_first(u):
            state = load(u)
        state = _softmax_unit(pending.pop(0), value_of(u), *state)
        if is_last(u):
            store(u, state)


def _split_v(v):
    lane = lax.broadcasted_iota(jnp.int32, v.shape, 1)
    zero = jnp.zeros_like(v)
    lo0 = jnp.where(lane < SWA_HEAD_DIM, v, zero)
    hi1 = jnp.where(lane >= SWA_HEAD_DIM, v, zero)
    swapped = jnp.concatenate([v[:, SWA_HEAD_DIM:], v[:, :SWA_HEAD_DIM]], axis=-1)
    hi0 = jnp.where(lane >= SWA_HEAD_DIM, swapped, zero)
    lo1 = jnp.where(lane < SWA_HEAD_DIM, swapped, zero)
    return ((lo0, hi0), (lo1, hi1))


def _local_kernel(sink_ref, zp_ref, zpp_ref, zpn_ref, q_ref, k_ref, kp_ref, kn_ref,
                  vt_ref, vtp_ref, vtn_ref, kc_ref, vtc_ref, pw_ref, ps_ref,
                  ypool_ref, yswa_ref, xe_ref, qt_ref, ke_ref, vte_ref, ot_ref, *, seq_len):
    j = pl.program_id(1)
    rows = MIX_ROWS
    t0 = j * rows
    first = j == 0
    last = j == pl.num_programs(1) - 1

    xe_ref[0:HALO, :] = jnp.where(first, 0.0, zpp_ref[...])
    xe_ref[HALO:HALO + rows, :] = zp_ref[...]
    xe_ref[HALO + rows:, :] = jnp.where(last, 0.0, zpn_ref[...])
    ypool_ref[...] = _pool_from_scratch(xe_ref, rows, t0, seq_len, pw_ref, ps_ref).astype(BF16)

    qt_ref[...] = q_ref[...].T
    ke_ref[0:BLOCK, :] = kp_ref[...]
    ke_ref[BLOCK:BLOCK + rows, :] = k_ref[...]
    ke_ref[BLOCK + rows:, :] = kn_ref[...]
    vte_ref[:, 0:BLOCK] = vtp_ref[0]
    vte_ref[:, BLOCK:BLOCK + rows] = vt_ref[0]
    vte_ref[:, BLOCK + rows:] = vtn_ref[0]

    grp = SWA_HEADS // SWA_KV_HEADS
    qw = grp * BLOCK
    shape3 = (BLOCK // 8, 8, qw)
    key_j = (lax.broadcasted_iota(jnp.int32, shape3, 0) * 8
             + lax.broadcasted_iota(jnp.int32, shape3, 1))
    qry_i = lax.broadcasted_iota(jnp.int32, shape3, 2) % BLOCK
    rel = key_j - qry_i
    lane = lax.broadcasted_iota(jnp.int32, (8, qw), 1)
    nqb = rows // BLOCK
    n_ctx_sub = kc_ref.shape[0] // BLOCK

    kinds = ["cur", "prev", "next"] + ["ctx%d" % c for c in range(n_ctx_sub)]
    units = [(qb, g, kind) for qb in range(nqb) for g in range(SWA_KV_HEADS) for kind in kinds]

    def q_tile(qb, g):
        parts = [qt_ref[(g * grp + e) * HEAD_PAD:(g * grp + e + 1) * HEAD_PAD,
                        qb * BLOCK:(qb + 1) * BLOCK] for e in range(grp)]
        return jnp.concatenate(parts, axis=1)

    def scores(u):
        qb, g, kind = u
        gcols = slice(g * HEAD_PAD, (g + 1) * HEAD_PAD)
        if kind.startswith("ctx"):
            c = int(kind[3:])
            return _softmax_scores(kc_ref[c * BLOCK:(c + 1) * BLOCK, gcols], q_tile(qb, g))
        off = {"prev": 0, "cur": 1, "next": 2}[kind]
        r0 = (qb + off) * BLOCK
        s3 = _softmax_scores(ke_ref[r0:r0 + BLOCK, gcols], q_tile(qb, g))
        if kind == "prev":
            ok = rel >= 0
            if qb == 0:
                ok = ok & jnp.logical_not(first)
            s3 = jnp.where(ok, s3, -jnp.inf)
        elif kind == "next":
            ok = rel <= 0
            if qb == nqb - 1:
                ok = ok & jnp.logical_not(last)
            s3 = jnp.where(ok, s3, -jnp.inf)
        return s3

    def values(u):
        qb, g, kind = u
        grows = slice(g * SWA_HEAD_DIM, (g + 1) * SWA_HEAD_DIM)
        if kind.startswith("ctx"):
            c = int(kind[3:])
            return vtc_ref[0, grows, c * BLOCK:(c + 1) * BLOCK]
        off = {"prev": 0, "cur": 1, "next": 2}[kind]
        return vte_ref[grows, (qb + off) * BLOCK:(qb + off + 1) * BLOCK]

    def load(u):
        return (jnp.full((8, qw), -jnp.inf, F32),
                jnp.zeros((SWA_HEAD_DIM + DENOM_ROWS, qw), F32))

    def store(u, state):
        qb, g, _ = u
        m, acc = state
        sink2 = jnp.where(lane < BLOCK, sink_ref[g * grp], sink_ref[g * grp + 1]) * LOG2E
        m_fin = jnp.maximum(m, sink2)
        a = jnp.exp2(m - m_fin)
        denom = acc[SWA_HEAD_DIM:SWA_HEAD_DIM + 1]
        l_tot = denom * a[0:1] + jnp.exp2(sink2 - m_fin)[0:1]
        o = acc[:SWA_HEAD_DIM] * (a[0:1] / l_tot)
        for e in range(grp):
            hq = g * grp + e
            ot_ref[hq * SWA_HEAD_DIM:(hq + 1) * SWA_HEAD_DIM, qb * BLOCK:(qb + 1) * BLOCK] = (
                o[:, e * BLOCK:(e + 1) * BLOCK])

    _softmax_pipeline(units, scores, values, load, store,
                      is_first=lambda u: u[2] == kinds[0], is_last=lambda u: u[2] == kinds[-1])
    yswa_ref[...] = ot_ref[...].T.astype(BF16)


def _local_mixers(sink, zpool, qs, ks, vst, lw, *, n_batch, seq_len, ctx_len):
    rows = MIX_ROWS
    spb = seq_len // rows
    n_lat = n_batch * seq_len
    ctx_blk0 = n_lat // ctx_len
    ctx_tile = n_lat // ROW_TILE
    cpt = ROW_TILE // ctx_len
    bpt = ROW_TILE // BLOCK

    main = lambda w: pl.BlockSpec((rows, w), lambda b, j, *_: (b * spb + j, 0))

    def prev_blk(unit):
        per = rows // unit
        return lambda b, j: jnp.maximum((b * spb + j) * per - 1, 0)

    def next_blk(unit):
        per = rows // unit
        last_blk = n_lat // unit - 1
        return lambda b, j: jnp.minimum((b * spb + j + 1) * per, last_blk)

    def row_halo(unit, w, blk):
        return pl.BlockSpec((unit, w), lambda b, j, *_: (blk(unit)(b, j), 0))

    def lane_halo(blk):
        f = blk(BLOCK)
        return pl.BlockSpec((1, vw, BLOCK), lambda b, j, *_: (f(b, j) // bpt, 0, f(b, j) % bpt))

    full = lambda a: pl.BlockSpec(a.shape, lambda b, j, *_: (0,) * a.ndim)
    kw = SWA_KV_HEADS * HEAD_PAD
    vw = SWA_KV_HEADS * SWA_HEAD_DIM
    qw = SWA_HEADS * HEAD_PAD
    grid_spec = pltpu.PrefetchScalarGridSpec(
        num_scalar_prefetch=1,
        grid=(n_batch, spb),
        in_specs=[
            main(D_POOL),
            row_halo(HALO, D_POOL, prev_blk),
            row_halo(HALO, D_POOL, next_blk),
            main(qw),
            main(kw),
            row_halo(BLOCK, kw, prev_blk),
            row_halo(BLOCK, kw, next_blk),
            pl.BlockSpec((1, vw, rows), lambda b, j, *_: (b * spb + j, 0, 0)),
            lane_halo(prev_blk),
            lane_halo(next_blk),
            pl.BlockSpec((ctx_len, kw), lambda b, j, *_: (ctx_blk0 + b, 0)),
            pl.BlockSpec((1, vw, ctx_len), lambda b, j, *_: (ctx_tile + b // cpt, 0, b % cpt)),
            full(lw["pool_w"]), full(lw["pool_scale"]),
        ],
        out_specs=[main(D_POOL), main(SWA_HEADS * SWA_HEAD_DIM)],
        scratch_shapes=[
            pltpu.VMEM((rows + 2 * HALO, D_POOL), F32),
            pltpu.VMEM((qw, rows), BF16),
            pltpu.VMEM((rows + 2 * BLOCK, kw), BF16),
            pltpu.VMEM((vw, rows + 2 * BLOCK), BF16),
            pltpu.VMEM((SWA_HEADS * SWA_HEAD_DIM, rows), F32),
        ],
    )
    return pl.pallas_call(
        functools.partial(_local_kernel, seq_len=seq_len),
        grid_spec=grid_spec,
        out_shape=[jax.ShapeDtypeStruct((zpool.shape[0], D_POOL), BF16),
                   jax.ShapeDtypeStruct((zpool.shape[0], SWA_HEADS * SWA_HEAD_DIM), BF16)],
        compiler_params=pltpu.CompilerParams(
            dimension_semantics=("arbitrary", "arbitrary"), vmem_limit_bytes=VMEM_LIMIT),
        name="local_mixers",
    )(sink, zpool, zpool, zpool, qs, ks, ks, ks, vst, vst, vst, ks, vst,
      lw["pool_w"], lw["pool_scale"])


def _ctx_kernel(sink_ref, zp_ref, q_ref, k_ref, v_ref, qm_ref, km_ref, vmt_ref, pw_ref, ps_ref,
                ypool_in, yswa_in, ymla_in, ypool_ref, yswa_ref, ymla_ref, xe_ref, *, ctx_len):
    del ypool_in, yswa_in, ymla_in
    rows = ctx_len
    xe_ref[0:HALO, :] = jnp.zeros((HALO, D_POOL), F32)
    xe_ref[HALO:HALO + rows, :] = zp_ref[...]
    xe_ref[HALO + rows:, :] = jnp.zeros((HALO, D_POOL), F32)
    ypool_ref[...] = _pool_from_scratch(xe_ref, rows, 0, ctx_len, pw_ref, ps_ref).astype(BF16)

    v_split = _split_v(v_ref[...])
    grp = SWA_HEADS // SWA_KV_HEADS
    for g in range(SWA_KV_HEADS):
        k = k_ref[:, g * HEAD_PAD:(g + 1) * HEAD_PAD]
        o = None
        for e in range(grp):
            hq = g * grp + e
            q = q_ref[:, hq * HEAD_PAD:(hq + 1) * HEAD_PAD]
            t = _sink_softmax_pv([_dot_nt(q, k)], [v_split[g][e]], sink_ref[hq] * LOG2E)
            o = t if o is None else o + t
        yswa_ref[:, g * LANE:(g + 1) * LANE] = o.astype(BF16)

    outs = []
    for hh in range(MLA_HEADS):
        q = qm_ref[:, hh * HEAD_PAD:(hh + 1) * HEAD_PAD]
        k = km_ref[:, hh * HEAD_PAD:(hh + 1) * HEAD_PAD]
        st = _dot_nt(k, q)
        m = jnp.max(st, axis=0, keepdims=True)
        p = jnp.exp2(st - m)
        l = jnp.sum(p, axis=0, keepdims=True)
        ot = _dot(vmt_ref[0, hh * MLA_V:(hh + 1) * MLA_V, :], p.astype(BF16))
        outs.append(ot / l)
    ymla_ref[...] = jnp.concatenate(outs, axis=0).T.astype(BF16)


def _ctx_mixers(sink, zpool, qs, ks, vs, qm, km, vmt, lw, ys, *, n_batch, seq_len, ctx_len):
    n_lat = n_batch * seq_len
    blk0 = n_lat // ctx_len
    ctx_tile = n_lat // ROW_TILE
    cpt = ROW_TILE // ctx_len
    rowb = lambda w: pl.BlockSpec((ctx_len, w), lambda b, *_: (blk0 + b, 0))
    outb = rowb
    full = lambda a: pl.BlockSpec(a.shape, lambda b, *_: (0,) * a.ndim)
    n_in = 10
    hw = MLA_HEADS * HEAD_PAD
    grid_spec = pltpu.PrefetchScalarGridSpec(
        num_scalar_prefetch=1,
        grid=(n_batch,),
        in_specs=[rowb(D_POOL), rowb(SWA_HEADS * HEAD_PAD), rowb(SWA_KV_HEADS * HEAD_PAD),
                  rowb(SWA_KV_HEADS * SWA_HEAD_DIM), rowb(hw), rowb(hw),
                  pl.BlockSpec((1, MLA_HEADS * MLA_V, ctx_len),
                               lambda b, *_: (ctx_tile + b // cpt, 0, b % cpt)),
                  full(lw["pool_w"]), full(lw["pool_scale"])]
                 + [pl.BlockSpec(memory_space=pl.ANY)] * len(ys),
        out_specs=[outb(D_POOL), outb(SWA_HEADS * SWA_HEAD_DIM), outb(MLA_HEADS * MLA_V)],
        scratch_shapes=[pltpu.VMEM((ctx_len + 2 * HALO, D_POOL), F32)],
    )
    return pl.pallas_call(
        functools.partial(_ctx_kernel, ctx_len=ctx_len),
        grid_spec=grid_spec,
        out_shape=[jax.ShapeDtypeStruct(y.shape, y.dtype) for y in ys],
        input_output_aliases={n_in + i: i for i in range(len(ys))},
        compiler_params=pltpu.CompilerParams(
            dimension_semantics=("arbitrary",), vmem_limit_bytes=VMEM_LIMIT),
        name="ctx_mixers",
    )(sink, zpool, qs, ks, vs, qm, km, vmt, lw["pool_w"], lw["pool_scale"], *ys)


def _mla_kernel(q_ref, k_ref, vt_ref, kc_ref, vtc_ref, o_ref, qt_ref, m_ref, acc_ref, ot_ref,
                *, n_chunks, ctx_len):
    tq = q_ref.shape[0]
    qt_ref[...] = q_ref[...].T
    m_ref[...] = jnp.full(m_ref.shape, -jnp.inf, F32)
    acc_ref[...] = jnp.zeros(acc_ref.shape, F32)

    hcols = lambda h: slice(h * HEAD_PAD, (h + 1) * HEAD_PAD)
    vrows = lambda h: slice(h * MLA_V, (h + 1) * MLA_V)
    ksub = lambda s: slice(s * MLA_KSUB, (s + 1) * MLA_KSUB)
    qcols = lambda qh: slice(qh * MLA_QSUB, (qh + 1) * MLA_QSUB)

    def run_chunk(k_of, vt_of, n_sub):
        units = [(qh, h, s) for qh in range(tq // MLA_QSUB) for h in range(MLA_HEADS)
                 for s in range(n_sub)]

        def scores(u):
            qh, h, s = u
            return _softmax_scores(k_of(h, s), qt_ref[hcols(h), qcols(qh)])

        def load(u):
            qh, h, _ = u
            return (m_ref[h, :, qcols(qh)], acc_ref[h, :, qcols(qh)])

        def store(u, state):
            qh, h, _ = u
            m_ref[h, :, qcols(qh)] = state[0]
            acc_ref[h, :, qcols(qh)] = state[1]

        _softmax_pipeline(units, scores, lambda u: vt_of(u[1], u[2]), load, store,
                          is_first=lambda u: u[2] == 0, is_last=lambda u: u[2] == n_sub - 1)

    run_chunk(lambda h, s: kc_ref[ksub(s), hcols(h)],
              lambda h, s: vtc_ref[0, vrows(h), ksub(s)],
              ctx_len // MLA_KSUB)

    spt = ROW_TILE // MLA_KSUB

    def body(c, carry):
        r0 = pl.multiple_of(c * (MLA_TILES_PER_ITER * ROW_TILE), ROW_TILE)
        run_chunk(lambda h, s: k_ref[pl.ds(r0 + s * MLA_KSUB, MLA_KSUB), hcols(h)],
                  lambda h, s: vt_ref[c * MLA_TILES_PER_ITER + s // spt, vrows(h), ksub(s % spt)],
                  MLA_TILES_PER_ITER * spt)
        return carry

    lax.fori_loop(0, n_chunks // MLA_TILES_PER_ITER, body, 0)

    for h in range(MLA_HEADS):
        ot_ref[vrows(h), :] = acc_ref[h, :MLA_V, :] / acc_ref[h, MLA_V:MLA_V + 1, :]
    o_ref[...] = ot_ref[...].T.astype(BF16)


def _mla_latent(qm, km, vmt, *, n_batch, seq_len, ctx_len):
    tq = MLA_TQ
    nq = seq_len // tq
    tps = seq_len // ROW_TILE
    n_lat = n_batch * seq_len
    hw = MLA_HEADS * HEAD_PAD
    vw = MLA_HEADS * MLA_V
    ctx_blk0 = n_lat // ctx_len
    ctx_tile = n_lat // ROW_TILE
    cpt = ROW_TILE // ctx_len
    once = pl.Buffered(1)
    return pl.pallas_call(
        functools.partial(_mla_kernel, n_chunks=tps, ctx_len=ctx_len),
        grid=(n_batch, nq),
        in_specs=[
            pl.BlockSpec((tq, hw), lambda b, i: (b * nq + i, 0)),
            pl.BlockSpec((seq_len, hw), lambda b, i: (b, 0), pipeline_mode=once),
            pl.BlockSpec((tps, vw, ROW_TILE), lambda b, i: (b, 0, 0), pipeline_mode=once),
            pl.BlockSpec((ctx_len, hw), lambda b, i: (ctx_blk0 + b, 0)),
            pl.BlockSpec((1, vw, ctx_len), lambda b, i: (ctx_tile + b // cpt, 0, b % cpt)),
        ],
        out_specs=pl.BlockSpec((tq, vw), lambda b, i: (b * nq + i, 0)),
        out_shape=jax.ShapeDtypeStruct((qm.shape[0], vw), BF16),
        scratch_shapes=[pltpu.VMEM((hw, tq), BF16),
                        pltpu.VMEM((MLA_HEADS, 8, tq), F32),
                        pltpu.VMEM((MLA_HEADS, MLA_V + DENOM_ROWS, tq), F32),
                        pltpu.VMEM((vw, tq), F32)],
        compiler_params=pltpu.CompilerParams(
            dimension_semantics=("arbitrary", "arbitrary"), vmem_limit_bytes=VMEM_LIMIT),
        name="mla_latent",
    )(qm, km, vmt, km, vmt)


def _outffn_kernel(x_ref, yp_ref, ys_ref, ym_ref, yg_ref, mod_ref, wout_ref, g1_ref, b1_ref,
                   w1_ref, w2_ref, g2_ref, b2_ref, o_ref, *, alpha):
    q = wout_ref.shape[0] // 4
    gate1 = mod_ref[0, 2:3, :]
    shift = mod_ref[0, 3:4, :]
    scale = mod_ref[0, 4:5, :]
    gate2 = mod_ref[0, 5:6, :]
    dff = w1_ref.shape[1]
    tm = x_ref.shape[0]
    groups = [slice(g * (tm // FFN_ROW_GROUPS), (g + 1) * (tm // FFN_ROW_GROUPS))
              for g in range(FFN_ROW_GROUPS)]
    ys = [(_dot(yp_ref[r, :], wout_ref[0:q, :]) + _dot(ys_ref[r, :], wout_ref[q:2 * q, :])
           + _dot(ym_ref[r, :], wout_ref[2 * q:3 * q, :]) + _dot(yg_ref[r, :], wout_ref[3 * q:, :]))
          for r in groups]
    x1s = [_ln(alpha * x_ref[r, :] + gate1 * y) * g1_ref[...] + b1_ref[...]
           for r, y in zip(groups, ys)]
    for r, x1 in zip(groups, x1s):
        h = (_ln(x1) * (1.0 + scale) + shift).astype(BF16)
        f = None
        for c in range(dff // FF_CHUNK):
            a = jnp.maximum(_dot(h, w1_ref[:, c * FF_CHUNK:(c + 1) * FF_CHUNK]), 0.0)
            t = _dot((a * a).astype(BF16), w2_ref[c * FF_CHUNK:(c + 1) * FF_CHUNK, :])
            f = t if f is None else f + t
        o_ref[r, :] = _ln(alpha * x1 + gate2 * f) * g2_ref[...] + b2_ref[...]


def _outffn(xs, ys, mods, lw, *, n_tiles, n_lat_tiles, tiles_per_seq, alpha):
    n_all, D = xs.shape
    tm = ROW_TILE
    n_batch = n_lat_tiles // tiles_per_seq

    def mod_map(i):
        return (jnp.where(i < n_lat_tiles, i // tiles_per_seq, n_batch), 0, 0)

    row = lambda w: pl.BlockSpec((tm, w), lambda i: (i, 0))
    const = lambda a: pl.BlockSpec(a.shape, lambda i: (0,) * a.ndim,
                                   pipeline_mode=pl.Buffered(1))
    consts1 = [lw["w_out"], lw["ln1_g"], lw["ln1_b"], lw["w_ff1"], lw["w_ff2"],
               lw["ln2_g"], lw["ln2_b"]]
    return pl.pallas_call(
        functools.partial(_outffn_kernel, alpha=alpha),
        grid=(n_tiles,),
        in_specs=[row(D)] + [row(256)] * 4 + [pl.BlockSpec((1, 8, D), mod_map)]
                 + [const(a) for a in consts1],
        out_specs=row(D),
        out_shape=jax.ShapeDtypeStruct((n_tiles * tm, D), F32),
        compiler_params=pltpu.CompilerParams(
            dimension_semantics=("arbitrary",), vmem_limit_bytes=VMEM_LIMIT),
        name="outproj_ffn",
    )(xs, *ys, mods, *consts1)


def _swap_halves(w, n_axes_groups, half):
    shp = w.shape
    w = w.reshape(shp[:-1] + (n_axes_groups, 2, half))
    w = jnp.stack([-w[..., 1, :], w[..., 0, :]], axis=-2)
    return w.reshape(shp)


def _pad_heads(w, heads, dim):
    shp = w.shape
    w = w.reshape(shp[:-1] + (heads, dim))
    w = jnp.pad(w, [(0, 0)] * (w.ndim - 1) + [(0, HEAD_PAD - dim)])
    return w.reshape(shp[:-1] + (heads * HEAD_PAD,))


def _prepare_weights(w_in, pool_w, pool_scale, mla_q_norm, mla_w_uq, mla_kv_norm, mla_w_ukv,
                     sgu_norm_g, sgu_norm_b, sgu_w, sgu_b, w_out, ln1_g, ln1_b, w_ff1, w_ff2,
                     ln2_g, ln2_b):
    L, D, _ = w_in.shape
    c = 0
    w_pool = w_in[..., c:c + 256]; c += 256
    w_q = w_in[..., c:c + 256] * (SWA_HEAD_DIM ** -0.5 * LOG2E); c += 256
    w_cq = w_in[..., c:c + 256]; c += 256
    w_sgu = w_in[..., c:c + 512]; c += 512
    w_k = w_in[..., c:c + 128]; c += 128
    w_v = w_in[..., c:c + 128]; c += 128
    w_ckv = w_in[..., c:c + 128]; c += 128
    w_kr = w_in[..., c:c + 32]
    rot = SWA_HEAD_DIM // 4
    w_qsw = _swap_halves(w_q, SWA_HEADS * 2, rot)
    w_ksw = _swap_halves(w_k, SWA_KV_HEADS * 2, rot)
    w_krsw = _swap_halves(w_kr, 2, MLA_ROPE // 4)

    def kr_pad(w):
        return jnp.pad(w, [(0, 0), (0, 0), (MLA_NOPE, HEAD_PAD - MLA_NOPE - MLA_ROPE)])

    w_in_ext = jnp.concatenate([
        w_pool, _pad_heads(w_q, SWA_HEADS, SWA_HEAD_DIM), _pad_heads(w_qsw, SWA_HEADS, SWA_HEAD_DIM),
        w_cq, w_sgu, _pad_heads(w_k, SWA_KV_HEADS, SWA_HEAD_DIM),
        _pad_heads(w_ksw, SWA_KV_HEADS, SWA_HEAD_DIM), w_v, w_ckv, kr_pad(w_kr), kr_pad(w_krsw),
    ], axis=-1).astype(BF16)
    assert w_in_ext.shape[-1] == D_IN_EXT

    qd = MLA_NOPE + MLA_ROPE
    wq = mla_w_uq.reshape(L, MLA_Q_RANK, MLA_HEADS, qd) * (qd ** -0.5 * LOG2E)
    wq_rope_sw = _swap_halves(wq[..., MLA_NOPE:], 2, MLA_ROPE // 4)
    pad_q = [(0, 0), (0, 0), (0, 0), (0, HEAD_PAD - qd)]
    wq_full = jnp.pad(wq, pad_q).reshape(L, MLA_Q_RANK, MLA_HEADS * HEAD_PAD)
    wq_sw = jnp.pad(wq_rope_sw, [(0, 0), (0, 0), (0, 0), (MLA_NOPE, HEAD_PAD - qd)])
    wq_sw = wq_sw.reshape(L, MLA_Q_RANK, MLA_HEADS * HEAD_PAD)
    w_uq_ext = jnp.concatenate([wq_full, wq_sw], axis=-1).astype(BF16)

    wkv = mla_w_ukv.reshape(L, MLA_KV_RANK, MLA_HEADS, MLA_NOPE + MLA_V)
    w_kn = jnp.pad(wkv[..., :MLA_NOPE], [(0, 0), (0, 0), (0, 0), (0, HEAD_PAD - MLA_NOPE)])
    w_kn = w_kn.reshape(L, MLA_KV_RANK, MLA_HEADS * HEAD_PAD).astype(BF16)
    w_vt = wkv[..., MLA_NOPE:].reshape(L, MLA_KV_RANK, MLA_HEADS * MLA_V)
    w_vt = jnp.swapaxes(w_vt, 1, 2).astype(BF16)

    g_n = len(POOL_WINDOWS)
    eye = jnp.eye(g_n, dtype=F32)
    pool_bd = (pool_w[:, :, :, None, :] * eye[None, :, None, :, None])
    pool_bd = pool_bd.reshape(L, g_n * POOL_GROUP, g_n * POOL_GROUP).astype(BF16)

    sgu_bias = jnp.repeat(jnp.swapaxes(sgu_b, 1, 2), SGU_HEAD_DIM, axis=-1)

    r2 = lambda a: a.reshape(L, 1, a.shape[-1])
    return dict(
        w_in=w_in_ext, q_norm=r2(mla_q_norm), w_uq=w_uq_ext, kv_norm=r2(mla_kv_norm),
        w_kn=w_kn, w_vt=w_vt, sgu_g=r2(sgu_norm_g), sgu_b=r2(sgu_norm_b),
        sgu_w=sgu_w.astype(BF16), sgu_bias=sgu_bias,
        pool_w=pool_bd, pool_scale=r2(pool_scale),
        w_out=w_out.astype(BF16), ln1_g=r2(ln1_g), ln1_b=r2(ln1_b),
        w_ff1=w_ff1.astype(BF16), w_ff2=w_ff2.astype(BF16), ln2_g=r2(ln2_g), ln2_b=r2(ln2_b),
    )


def _rope_tables(n):
    pos = np.arange(n)
    row = (pos // GRID_W).astype(np.float32)
    col = (pos % GRID_W).astype(np.float32)

    def head_table(d_rot, lead, width):
        d_ax = d_rot // 2
        inv = ROPE_BASE ** (-jnp.arange(0, d_ax, 2, dtype=F32) / d_ax)
        ar = jnp.asarray(row)[:, None] * inv
        ac = jnp.asarray(col)[:, None] * inv
        cos = jnp.concatenate([jnp.cos(ar), jnp.cos(ar), jnp.cos(ac), jnp.cos(ac)], axis=-1)
        sin = jnp.concatenate([jnp.sin(ar), jnp.sin(ar), jnp.sin(ac), jnp.sin(ac)], axis=-1)
        tail = width - lead - d_rot
        cos = jnp.pad(cos, [(0, 0), (lead, tail)], constant_values=1.0)
        sin = jnp.pad(sin, [(0, 0), (lead, tail)])
        ident = (jnp.ones((ROW_TILE, width), F32), jnp.zeros((ROW_TILE, width), F32))
        return jnp.concatenate([cos, ident[0]], axis=0), jnp.concatenate([sin, ident[1]], axis=0)

    cos_h, sin_h = head_table(SWA_HEAD_DIM, 0, HEAD_PAD)
    cos_m, sin_m = head_table(MLA_ROPE, MLA_NOPE, HEAD_PAD)
    return cos_h, sin_h, cos_m, sin_m


def kernel(x, c, ctx, c_ctx, w_ada, b_ada, w_in, pool_w, pool_scale, swa_sink, mla_q_norm,
           mla_w_uq, mla_kv_norm, mla_w_ukv, sgu_norm_g, sgu_norm_b, sgu_w, sgu_b, w_out,
           ln1_g, ln1_b, w_ff1, w_ff2, ln2_g, ln2_b):
    B, n, D = x.shape
    C = ctx.shape[1]
    L = w_in.shape[0]
    assert n % (MLA_TILES_PER_ITER * ROW_TILE) == 0 and MIX_ROWS == ROW_TILE and n % MLA_TQ == 0
    assert MLA_TQ % MLA_QSUB == 0 and ROW_TILE % MLA_KSUB == 0 and C % MLA_KSUB == 0
    assert (B * C) % ROW_TILE == 0 and ROW_TILE % C == 0 and C % BLOCK == 0 and B + 1 <= 8
    alpha = (2 * L) ** 0.25
    n_lat = B * n
    n_lat_tiles = n_lat // ROW_TILE
    n_all_tiles = (n_lat + B * C) // ROW_TILE
    tiles_per_seq = n // ROW_TILE

    cond = jnp.concatenate([c, c_ctx[None, :], jnp.zeros((8 - B - 1, D), F32)], axis=0)
    mods_all = _modulation(cond, w_ada, b_ada)
    mods_all = jnp.pad(mods_all.reshape(L, 8, N_MOD, D),
                       [(0, 0), (0, 0), (0, 8 - N_MOD), (0, 0)])

    W = _prepare_weights(w_in, pool_w, pool_scale, mla_q_norm, mla_w_uq, mla_kv_norm, mla_w_ukv,
                         sgu_norm_g, sgu_norm_b, sgu_w, sgu_b, w_out, ln1_g, ln1_b, w_ff1,
                         w_ff2, ln2_g, ln2_b)
    tabs = _rope_tables(n)

    xs = jnp.concatenate([x.reshape(n_lat, D), ctx.reshape(B * C, D)], axis=0)
    for l in range(L):
        last = l == L - 1
        lw = {k: v[l] for k, v in W.items()}
        mods = mods_all[l]
        sink = swa_sink[l]
        zpool, qs, ks, vs, vst, qm, km, vmt, ysgu = _inproj(
            xs, mods, tabs, lw, n_lat_tiles=n_lat_tiles, tiles_per_seq=tiles_per_seq)
        ypool, yswa = _local_mixers(sink, zpool, qs, ks, vst, lw,
                                    n_batch=B, seq_len=n, ctx_len=C)
        ymla = _mla_latent(qm, km, vmt, n_batch=B, seq_len=n, ctx_len=C)
        if last:
            n_tiles = n_lat_tiles
        else:
            ypool, yswa, ymla = _ctx_mixers(sink, zpool, qs, ks, vs, qm, km, vmt, lw,
                                            (ypool, yswa, ymla), n_batch=B, seq_len=n, ctx_len=C)
            n_tiles = n_all_tiles
        xs = _outffn(xs, (ypool, yswa, ymla, ysgu), mods, lw, n_tiles=n_tiles,
                     n_lat_tiles=n_lat_tiles, tiles_per_seq=tiles_per_seq, alpha=alpha)
    return xs[:n_lat].reshape(B, n, D)
```

```python
import functools
import math

import jax
import jax.numpy as jnp
import numpy as np
from jax import lax
from jax.experimental import pallas as pl
from jax.experimental.pallas import tpu as pltpu

F32 = jnp.float32
BF16 = jnp.bfloat16

GRID_W = 64
POOL_WINDOWS = (2, 4, 8, 16)
POOL_GROUP = 64
D_POOL = 256
SWA_HEADS = 4
SWA_KV_HEADS = 2
SWA_HEAD_DIM = 64
SWA_WINDOW = 128
BLOCK = 128
MLA_HEADS = 4
MLA_NOPE = 64
MLA_ROPE = 32
MLA_V = 64
MLA_Q_RANK = 256
MLA_KV_RANK = 128
SGU_HEADS = 4
SGU_WIDTH = 256
SGU_HEAD_DIM = 64
SGU_CHUNK = 128
ROPE_BASE = 10000.0
N_MOD = 6
EPS = 1e-6

LANE = 128
ROW_TILE = 512
MIX_ROWS = 512
HALO = 8
MLA_TQ = 1024
MLA_QSUB = 256
MLA_KSUB = 256
DENOM_ROWS = 16
MLA_TILES_PER_ITER = 4
SOFTMAX_AHEAD = 8
LOG2E = math.log2(math.e)
FF_CHUNK = 1024
INPROJ_ROW_GROUPS = 1
FFN_ROW_GROUPS = 2
VMEM_LIMIT = 56 * 1024 * 1024

HEAD_PAD = LANE

G_POOL = (0, 256)
G_Q = (256, 256)
G_QSW = (512, 256)
G_CQ = (768, 256)
G_SGU = (1024, 512)
G_K = (1536, 128)
G_KSW = (1664, 128)
G_V = (1792, 128)
G_CKV = (1920, 128)
G_KR = (2048, 128)
G_KRSW = (2176, 128)
D_IN_EXT = 2304


def _ln(x):
    mu = jnp.mean(x, axis=-1, keepdims=True)
    xc = x - mu
    var = jnp.mean(xc * xc, axis=-1, keepdims=True)
    return xc * lax.rsqrt(var + EPS)


def _dot(a, b):
    return jnp.dot(a, b, preferred_element_type=F32)


def _dot_nt(a, b):
    return lax.dot_general(a, b, (((1,), (1,)), ((), ())), preferred_element_type=F32)


def _gelu(x):
    return 0.5 * x * (1.0 + lax.erf(x * (1.0 / math.sqrt(2.0))))


def _mod_kernel(c_ref, w_ref, b_ref, o_ref):
    c = c_ref[...]
    s = c * (1.0 / (1.0 + jnp.exp(-c)))
    o_ref[0] = jnp.dot(s, w_ref[0], preferred_element_type=F32,
                       precision=lax.Precision.HIGHEST) + b_ref[0]


def _modulation(cond, w_ada, b_ada):
    L, D, D6 = w_ada.shape
    nc = D6 // D
    return pl.pallas_call(
        _mod_kernel,
        grid=(L, nc),
        in_specs=[
            pl.BlockSpec((8, D), lambda l, j: (0, 0)),
            pl.BlockSpec((1, D, D), lambda l, j: (l, 0, j)),
            pl.BlockSpec((1, 1, D), lambda l, j: (l, 0, j)),
        ],
        out_specs=pl.BlockSpec((1, 8, D), lambda l, j: (l, 0, j)),
        out_shape=jax.ShapeDtypeStruct((L, 8, D6), F32),
        compiler_params=pltpu.CompilerParams(
            dimension_semantics=("arbitrary", "arbitrary"), vmem_limit_bytes=VMEM_LIMIT),
        name="adaln_modulation",
    )(cond, w_ada, b_ada.reshape(L, 1, D6))


def _inproj_kernel(x_ref, mod_ref, cq_ref, sq_ref, cm_ref, sm_ref, win_ref,
                   qg_ref, wuq_ref, kvg_ref, wkn_ref, wvt_ref,
                   sg_ref, sb_ref, sw_ref, sbias_ref,
                   zpool_ref, qs_ref, ks_ref, vs_ref, vst_ref, qm_ref, km_ref, vmt_ref, ysgu_ref):
    shift = mod_ref[0, 0:1, :]
    scale = mod_ref[0, 1:2, :]
    tm = x_ref.shape[0]
    gr = tm // INPROJ_ROW_GROUPS
    for grp_i in range(INPROJ_ROW_GROUPS):
        r = slice(grp_i * gr, (grp_i + 1) * gr)
        h = (_ln(x_ref[r, :]) * (1.0 + scale) + shift).astype(BF16)

        def proj(g, h=h):
            return _dot(h, win_ref[:, g[0]:g[0] + g[1]])

        zpool_ref[r, :] = proj(G_POOL)

        cos_k = cq_ref[r, :]
        sin_k = sq_ref[r, :]
        reps = SWA_HEADS // SWA_KV_HEADS
        cos_q = jnp.concatenate([cos_k] * reps, axis=-1)
        sin_q = jnp.concatenate([sin_k] * reps, axis=-1)
        qs_ref[r, :] = (proj(G_Q) * cos_q + proj(G_QSW) * sin_q).astype(BF16)
        ks_ref[r, :] = (proj(G_K) * cos_k + proj(G_KSW) * sin_k).astype(BF16)
        v_swa = proj(G_V)
        vs_ref[r, :] = v_swa.astype(BF16)
        vst_ref[0, :, r] = v_swa.T.astype(BF16)

        cos_m = cm_ref[r, :]
        sin_m = sm_ref[r, :]
        cq = proj(G_CQ)
        cqn = cq * lax.rsqrt(jnp.mean(cq * cq, axis=-1, keepdims=True) + EPS) * qg_ref[...]
        q2 = _dot(cqn.astype(BF16), wuq_ref[...])
        hw = MLA_HEADS * HEAD_PAD
        cos_m4 = jnp.concatenate([cos_m] * MLA_HEADS, axis=-1)
        sin_m4 = jnp.concatenate([sin_m] * MLA_HEADS, axis=-1)
        qm_ref[r, :] = (q2[:, :hw] * cos_m4 + q2[:, hw:] * sin_m4).astype(BF16)

        ckv = proj(G_CKV)
        ckvn = (ckv * lax.rsqrt(jnp.mean(ckv * ckv, axis=-1, keepdims=True) + EPS)
                * kvg_ref[...]).astype(BF16)
        kr = proj(G_KR) * cos_m + proj(G_KRSW) * sin_m
        kn = _dot(ckvn, wkn_ref[...])
        km_ref[r, :] = (kn + jnp.concatenate([kr] * MLA_HEADS, axis=-1)).astype(BF16)
        vmt_ref[0, :, r] = _dot_nt(wvt_ref[...], ckvn).astype(BF16)

        z = _gelu(proj(G_SGU))
        u = z[:, :SGU_WIDTH]
        v = _ln(z[:, SGU_WIDTH:]) * sg_ref[...] + sb_ref[...]
        vb = v.astype(BF16)
        for c in range(gr // SGU_CHUNK):
            r0 = c * SGU_CHUNK
            cols = []
            for hh in range(SGU_HEADS):
                c0 = hh * SGU_HEAD_DIM
                cols.append(_dot(sw_ref[hh], vb[r0:r0 + SGU_CHUNK, c0:c0 + SGU_HEAD_DIM]))
            mixed = jnp.concatenate(cols, axis=-1) + sbias_ref[...]
            ysgu_ref[grp_i * gr + r0:grp_i * gr + r0 + SGU_CHUNK, :] = (
                u[r0:r0 + SGU_CHUNK, :] * mixed).astype(BF16)


def _inproj(xs, mods, tabs, lw, *, n_lat_tiles, tiles_per_seq):
    n_all, D = xs.shape
    tm = ROW_TILE
    nt = n_all // tm
    n_batch = n_lat_tiles // tiles_per_seq

    def mod_map(i):
        return (jnp.where(i < n_lat_tiles, i // tiles_per_seq, n_batch), 0, 0)

    def pos_map(i):
        return (jnp.where(i < n_lat_tiles, i % tiles_per_seq, tiles_per_seq), 0)

    row = lambda w: pl.BlockSpec((tm, w), lambda i: (i, 0))
    full = lambda a: pl.BlockSpec(a.shape, lambda i: (0,) * a.ndim)
    cos_q, sin_q, cos_m, sin_m = tabs
    consts = [lw["w_in"], lw["q_norm"], lw["w_uq"], lw["kv_norm"], lw["w_kn"], lw["w_vt"],
              lw["sgu_g"], lw["sgu_b"], lw["sgu_w"], lw["sgu_bias"]]
    hw = MLA_HEADS * HEAD_PAD
    out_shape = [
        jax.ShapeDtypeStruct((n_all, D_POOL), F32),
        jax.ShapeDtypeStruct((n_all, SWA_HEADS * SWA_HEAD_DIM), BF16),
        jax.ShapeDtypeStruct((n_all, SWA_KV_HEADS * SWA_HEAD_DIM), BF16),
        jax.ShapeDtypeStruct((n_all, SWA_KV_HEADS * SWA_HEAD_DIM), BF16),
        jax.ShapeDtypeStruct((nt, SWA_KV_HEADS * SWA_HEAD_DIM, tm), BF16),
        jax.ShapeDtypeStruct((n_all, hw), BF16),
        jax.ShapeDtypeStruct((n_all, hw), BF16),
        jax.ShapeDtypeStruct((nt, MLA_HEADS * MLA_V, tm), BF16),
        jax.ShapeDtypeStruct((n_all, SGU_WIDTH), BF16),
    ]
    out_specs = [row(D_POOL), row(SWA_HEADS * SWA_HEAD_DIM), row(SWA_KV_HEADS * SWA_HEAD_DIM),
                 row(SWA_KV_HEADS * SWA_HEAD_DIM),
                 pl.BlockSpec((1, SWA_KV_HEADS * SWA_HEAD_DIM, tm), lambda i: (i, 0, 0)),
                 row(hw), row(hw),
                 pl.BlockSpec((1, MLA_HEADS * MLA_V, tm), lambda i: (i, 0, 0)),
                 row(SGU_WIDTH)]
    return pl.pallas_call(
        _inproj_kernel,
        grid=(nt,),
        in_specs=[row(D),
                  pl.BlockSpec((1, 8, D), mod_map),
                  pl.BlockSpec((tm, cos_q.shape[1]), pos_map),
                  pl.BlockSpec((tm, sin_q.shape[1]), pos_map),
                  pl.BlockSpec((tm, cos_m.shape[1]), pos_map),
                  pl.BlockSpec((tm, sin_m.shape[1]), pos_map)] + [full(a) for a in consts],
        out_specs=out_specs,
        out_shape=out_shape,
        compiler_params=pltpu.CompilerParams(
            dimension_semantics=("arbitrary",), vmem_limit_bytes=VMEM_LIMIT),
        name="inproj",
    )(xs, mods, cos_q, sin_q, cos_m, sin_m, *consts)


def _pool_from_scratch(xe_ref, rows, t0, seq_len, pw_ref, ps_ref):
    def win(d):
        return xe_ref[HALO + d:HALO + d + rows, :]

    x = win(0)
    t = t0 + lax.broadcasted_iota(jnp.int32, (rows, 1), 0)
    lane_group = lax.broadcasted_iota(jnp.int32, (1, D_POOL), 1) // POOL_GROUP
    s = x
    lo_d, hi_d = 0, 0
    mean = jnp.zeros_like(x)
    for gi, w in enumerate(POOL_WINDOWS):
        for d in range(-(w // 2), lo_d):
            s = s + win(d)
        for d in range(hi_d + 1, w // 2):
            s = s + win(d)
        lo_d, hi_d = -(w // 2), w // 2 - 1
        cnt = (jnp.minimum(t + w // 2, seq_len) - jnp.maximum(t - w // 2, 0)).astype(F32)
        mean = jnp.where(lane_group == gi, s / cnt, mean)
    d = (mean - x).astype(BF16)
    return _dot(d, pw_ref[...]) * ps_ref[...]


def _sink_softmax_pv(s_list, v_list, sink2):
    m = sink2
    for s in s_list:
        m = jnp.maximum(m, jnp.max(s, axis=-1, keepdims=True))
    denom = jnp.exp2(sink2 - m)
    o = None
    for s, v in zip(s_list, v_list):
        p = jnp.exp2(s - m)
        denom = denom + jnp.sum(p, axis=-1, keepdims=True)
        t = _dot(p.astype(BF16), v)
        o = t if o is None else o + t
    return o / denom


def _sublane_allmax(x):
    for shift in (4, 2, 1):
        x = jnp.maximum(x, pltpu.roll(x, shift, axis=0))
    return x


def _softmax_scores(k_sub, qt):
    return _dot(k_sub, qt).reshape(k_sub.shape[0] // 8, 8, qt.shape[1])


def _softmax_unit(s3, vt_sub, m, acc):
    ks = s3.shape[0] * 8
    qs = s3.shape[2]
    nv = acc.shape[0]
    m_new = jnp.maximum(m, _sublane_allmax(jnp.max(s3, axis=0)))
    alpha = jnp.exp2(m - m_new)
    p = jnp.exp2(s3 - m_new[None]).reshape(ks, qs).astype(BF16)
    vt_ext = jnp.concatenate([vt_sub, jnp.ones((DENOM_ROWS, ks), BF16)], axis=0)
    acc = (acc.reshape(nv // 8, 8, qs) * alpha[None]).reshape(nv, qs) + _dot(vt_ext, p)
    return m_new, acc


def _softmax_pipeline(units, score_of, value_of, load, store, *, is_first, is_last):
    pending = [score_of(u) for u in units[:SOFTMAX_AHEAD]]
    state = None
    for i, u in enumerate(units):
        if i + SOFTMAX_AHEAD < len(units):
            pending.append(score_of(units[i + SOFTMAX_AHEAD]))
        if is_first(u):
            state = load(u)
        state = _softmax_unit(pending.pop(0), value_of(u), *state)
        if is_last(u):
            store(u, state)


def _split_v(v):
    lane = lax.broadcasted_iota(jnp.int32, v.shape, 1)
    zero = jnp.zeros_like(v)
    lo0 = jnp.where(lane < SWA_HEAD_DIM, v, zero)
    hi1 = jnp.where(lane >= SWA_HEAD_DIM, v, zero)
    swapped = jnp.concatenate([v[:, SWA_HEAD_DIM:], v[:, :SWA_HEAD_DIM]], axis=-1)
    hi0 = jnp.where(lane >= SWA_HEAD_DIM, swapped, zero)
    lo1 = jnp.where(lane < SWA_HEAD_DIM, swapped, zero)
    return ((lo0, hi0), (lo1, hi1))


def _local_kernel(sink_ref, zp_ref, zpp_ref, zpn_ref, q_ref, k_ref, kp_ref, kn_ref,
                  vt_ref, vtp_ref, vtn_ref, kc_ref, vtc_ref, pw_ref, ps_ref,
                  ypool_ref, yswa_ref, xe_ref, qt_ref, ke_ref, vte_ref, ot_ref, *, seq_len):
    j = pl.program_id(1)
    rows = MIX_ROWS
    t0 = j * rows
    first = j == 0
    last = j == pl.num_programs(1) - 1

    xe_ref[0:HALO, :] = jnp.where(first, 0.0, zpp_ref[...])
    xe_ref[HALO:HALO + rows, :] = zp_ref[...]
    xe_ref[HALO + rows:, :] = jnp.where(last, 0.0, zpn_ref[...])
    ypool_ref[...] = _pool_from_scratch(xe_ref, rows, t0, seq_len, pw_ref, ps_ref).astype(BF16)

    qt_ref[...] = q_ref[...].T
    ke_ref[0:BLOCK, :] = kp_ref[...]
    ke_ref[BLOCK:BLOCK + rows, :] = k_ref[...]
    ke_ref[BLOCK + rows:, :] = kn_ref[...]
    vte_ref[:, 0:BLOCK] = vtp_ref[0]
    vte_ref[:, BLOCK:BLOCK + rows] = vt_ref[0]
    vte_ref[:, BLOCK + rows:] = vtn_ref[0]

    grp = SWA_HEADS // SWA_KV_HEADS
    qw = grp * BLOCK
    shape3 = (BLOCK // 8, 8, qw)
    key_j = (lax.broadcasted_iota(jnp.int32, shape3, 0) * 8
             + lax.broadcasted_iota(jnp.int32, shape3, 1))
    qry_i = lax.broadcasted_iota(jnp.int32, shape3, 2) % BLOCK
    rel = key_j - qry_i
    lane = lax.broadcasted_iota(jnp.int32, (8, qw), 1)
    nqb = rows // BLOCK
    n_ctx_sub = kc_ref.shape[0] // BLOCK

    kinds = ["cur", "prev", "next"] + ["ctx%d" % c for c in range(n_ctx_sub)]
    units = [(qb, g, kind) for qb in range(nqb) for g in range(SWA_KV_HEADS) for kind in kinds]

    def q_tile(qb, g):
        zero = jnp.zeros((SWA_HEAD_DIM, BLOCK), BF16)
        cols = []
        for e in range(grp):
            hq = g * grp + e
            qh = qt_ref[hq * SWA_HEAD_DIM:(hq + 1) * SWA_HEAD_DIM, qb * BLOCK:(qb + 1) * BLOCK]
            cols.append(jnp.concatenate([qh if gg == g else zero
                                         for gg in range(SWA_KV_HEADS)], axis=0))
        return jnp.concatenate(cols, axis=1)

    def scores(u):
        qb, g, kind = u
        if kind.startswith("ctx"):
            c = int(kind[3:])
            return _softmax_scores(kc_ref[c * BLOCK:(c + 1) * BLOCK, :], q_tile(qb, g))
        off = {"prev": 0, "cur": 1, "next": 2}[kind]
        r0 = (qb + off) * BLOCK
        s3 = _softmax_scores(ke_ref[r0:r0 + BLOCK, :], q_tile(qb, g))
        if kind == "prev":
            ok = rel >= 0
            if qb == 0:
                ok = ok & jnp.logical_not(first)
            s3 = jnp.where(ok, s3, -jnp.inf)
        elif kind == "next":
            ok = rel <= 0
            if qb == nqb - 1:
                ok = ok & jnp.logical_not(last)
            s3 = jnp.where(ok, s3, -jnp.inf)
        return s3

    def values(u):
        qb, g, kind = u
        grows = slice(g * SWA_HEAD_DIM, (g + 1) * SWA_HEAD_DIM)
        if kind.startswith("ctx"):
            c = int(kind[3:])
            return vtc_ref[0, grows, c * BLOCK:(c + 1) * BLOCK]
        off = {"prev": 0, "cur": 1, "next": 2}[kind]
        return vte_ref[grows, (qb + off) * BLOCK:(qb + off + 1) * BLOCK]

    def load(u):
        return (jnp.full((8, qw), -jnp.inf, F32),
                jnp.zeros((SWA_HEAD_DIM + DENOM_ROWS, qw), F32))

    def store(u, state):
        qb, g, _ = u
        m, acc = state
        sink2 = jnp.where(lane < BLOCK, sink_ref[g * grp], sink_ref[g * grp + 1]) * LOG2E
        m_fin = jnp.maximum(m, sink2)
        a = jnp.exp2(m - m_fin)
        denom = acc[SWA_HEAD_DIM:SWA_HEAD_DIM + 1]
        l_tot = denom * a[0:1] + jnp.exp2(sink2 - m_fin)[0:1]
        o = acc[:SWA_HEAD_DIM] * (a[0:1] / l_tot)
        for e in range(grp):
            hq = g * grp + e
            ot_ref[hq * SWA_HEAD_DIM:(hq + 1) * SWA_HEAD_DIM, qb * BLOCK:(qb + 1) * BLOCK] = (
                o[:, e * BLOCK:(e + 1) * BLOCK])

    _softmax_pipeline(units, scores, values, load, store,
                      is_first=lambda u: u[2] == kinds[0], is_last=lambda u: u[2] == kinds[-1])
    yswa_ref[...] = ot_ref[...].T.astype(BF16)


def _local_mixers(sink, zpool, qs, ks, vst, lw, *, n_batch, seq_len, ctx_len):
    rows = MIX_ROWS
    spb = seq_len // rows
    n_lat = n_batch * seq_len
    ctx_blk0 = n_lat // ctx_len
    ctx_tile = n_lat // ROW_TILE
    cpt = ROW_TILE // ctx_len
    bpt = ROW_TILE // BLOCK

    main = lambda w: pl.BlockSpec((rows, w), lambda b, j, *_: (b * spb + j, 0))

    def prev_blk(unit):
        per = rows // unit
        return lambda b, j: jnp.maximum((b * spb + j) * per - 1, 0)

    def next_blk(unit):
        per = rows // unit
        last_blk = n_lat // unit - 1
        return lambda b, j: jnp.minimum((b * spb + j + 1) * per, last_blk)

    def row_halo(unit, w, blk):
        return pl.BlockSpec((unit, w), lambda b, j, *_: (blk(unit)(b, j), 0))

    def lane_halo(blk):
        f = blk(BLOCK)
        return pl.BlockSpec((1, vw, BLOCK), lambda b, j, *_: (f(b, j) // bpt, 0, f(b, j) % bpt))

    full = lambda a: pl.BlockSpec(a.shape, lambda b, j, *_: (0,) * a.ndim)
    kw = SWA_KV_HEADS * SWA_HEAD_DIM
    vw = SWA_KV_HEADS * SWA_HEAD_DIM
    qw = SWA_HEADS * SWA_HEAD_DIM
    grid_spec = pltpu.PrefetchScalarGridSpec(
        num_scalar_prefetch=1,
        grid=(n_batch, spb),
        in_specs=[
            main(D_POOL),
            row_halo(HALO, D_POOL, prev_blk),
            row_halo(HALO, D_POOL, next_blk),
            main(qw),
            main(kw),
            row_halo(BLOCK, kw, prev_blk),
            row_halo(BLOCK, kw, next_blk),
            pl.BlockSpec((1, vw, rows), lambda b, j, *_: (b * spb + j, 0, 0)),
            lane_halo(prev_blk),
            lane_halo(next_blk),
            pl.BlockSpec((ctx_len, kw), lambda b, j, *_: (ctx_blk0 + b, 0)),
            pl.BlockSpec((1, vw, ctx_len), lambda b, j, *_: (ctx_tile + b // cpt, 0, b % cpt)),
            full(lw["pool_w"]), full(lw["pool_scale"]),
        ],
        out_specs=[main(D_POOL), main(SWA_HEADS * SWA_HEAD_DIM)],
        scratch_shapes=[
            pltpu.VMEM((rows + 2 * HALO, D_POOL), F32),
            pltpu.VMEM((qw, rows), BF16),
            pltpu.VMEM((rows + 2 * BLOCK, kw), BF16),
            pltpu.VMEM((vw, rows + 2 * BLOCK), BF16),
            pltpu.VMEM((SWA_HEADS * SWA_HEAD_DIM, rows), F32),
        ],
    )
    return pl.pallas_call(
        functools.partial(_local_kernel, seq_len=seq_len),
        grid_spec=grid_spec,
        out_shape=[jax.ShapeDtypeStruct((zpool.shape[0], D_POOL), BF16),
                   jax.ShapeDtypeStruct((zpool.shape[0], SWA_HEADS * SWA_HEAD_DIM), BF16)],
        compiler_params=pltpu.CompilerParams(
            dimension_semantics=("arbitrary", "arbitrary"), vmem_limit_bytes=VMEM_LIMIT),
        name="local_mixers",
    )(sink, zpool, zpool, zpool, qs, ks, ks, ks, vst, vst, vst, ks, vst,
      lw["pool_w"], lw["pool_scale"])


def _ctx_kernel(sink_ref, zp_ref, q_ref, k_ref, v_ref, qm_ref, km_ref, vmt_ref, pw_ref, ps_ref,
                ypool_in, yswa_in, ymla_in, ypool_ref, yswa_ref, ymla_ref, xe_ref, *, ctx_len):
    del ypool_in, yswa_in, ymla_in
    rows = ctx_len
    xe_ref[0:HALO, :] = jnp.zeros((HALO, D_POOL), F32)
    xe_ref[HALO:HALO + rows, :] = zp_ref[...]
    xe_ref[HALO + rows:, :] = jnp.zeros((HALO, D_POOL), F32)
    ypool_ref[...] = _pool_from_scratch(xe_ref, rows, 0, ctx_len, pw_ref, ps_ref).astype(BF16)

    v_split = _split_v(v_ref[...])
    grp = SWA_HEADS // SWA_KV_HEADS
    for g in range(SWA_KV_HEADS):
        k = k_ref[:, g * SWA_HEAD_DIM:(g + 1) * SWA_HEAD_DIM]
        o = None
        for e in range(grp):
            hq = g * grp + e
            q = q_ref[:, hq * SWA_HEAD_DIM:(hq + 1) * SWA_HEAD_DIM]
            t = _sink_softmax_pv([_dot_nt(q, k)], [v_split[g][e]], sink_ref[hq] * LOG2E)
            o = t if o is None else o + t
        yswa_ref[:, g * LANE:(g + 1) * LANE] = o.astype(BF16)

    outs = []
    for hh in range(MLA_HEADS):
        q = qm_ref[:, hh * HEAD_PAD:(hh + 1) * HEAD_PAD]
        k = km_ref[:, hh * HEAD_PAD:(hh + 1) * HEAD_PAD]
        st = _dot_nt(k, q)
        m = jnp.max(st, axis=0, keepdims=True)
        p = jnp.exp2(st - m)
        l = jnp.sum(p, axis=0, keepdims=True)
        ot = _dot(vmt_ref[0, hh * MLA_V:(hh + 1) * MLA_V, :], p.astype(BF16))
        outs.append(ot / l)
    ymla_ref[...] = jnp.concatenate(outs, axis=0).T.astype(BF16)


def _ctx_mixers(sink, zpool, qs, ks, vs, qm, km, vmt, lw, ys, *, n_batch, seq_len, ctx_len):
    n_lat = n_batch * seq_len
    blk0 = n_lat // ctx_len
    ctx_tile = n_lat // ROW_TILE
    cpt = ROW_TILE // ctx_len
    rowb = lambda w: pl.BlockSpec((ctx_len, w), lambda b, *_: (blk0 + b, 0))
    outb = rowb
    full = lambda a: pl.BlockSpec(a.shape, lambda b, *_: (0,) * a.ndim)
    n_in = 10
    hw = MLA_HEADS * HEAD_PAD
    grid_spec = pltpu.PrefetchScalarGridSpec(
        num_scalar_prefetch=1,
        grid=(n_batch,),
        in_specs=[rowb(D_POOL), rowb(SWA_HEADS * SWA_HEAD_DIM), rowb(SWA_KV_HEADS * SWA_HEAD_DIM),
                  rowb(SWA_KV_HEADS * SWA_HEAD_DIM), rowb(hw), rowb(hw),
                  pl.BlockSpec((1, MLA_HEADS * MLA_V, ctx_len),
                               lambda b, *_: (ctx_tile + b // cpt, 0, b % cpt)),
                  full(lw["pool_w"]), full(lw["pool_scale"])]
                 + [pl.BlockSpec(memory_space=pl.ANY)] * len(ys),
        out_specs=[outb(D_POOL), outb(SWA_HEADS * SWA_HEAD_DIM), outb(MLA_HEADS * MLA_V)],
        scratch_shapes=[pltpu.VMEM((ctx_len + 2 * HALO, D_POOL), F32)],
    )
    return pl.pallas_call(
        functools.partial(_ctx_kernel, ctx_len=ctx_len),
        grid_spec=grid_spec,
        out_shape=[jax.ShapeDtypeStruct(y.shape, y.dtype) for y in ys],
        input_output_aliases={n_in + i: i for i in range(len(ys))},
        compiler_params=pltpu.CompilerParams(
            dimension_semantics=("arbitrary",), vmem_limit_bytes=VMEM_LIMIT),
        name="ctx_mixers",
    )(sink, zpool, qs, ks, vs, qm, km, vmt, lw["pool_w"], lw["pool_scale"], *ys)


def _mla_kernel(q_ref, k_ref, vt_ref, kc_ref, vtc_ref, o_ref, qt_ref, m_ref, acc_ref, ot_ref,
                *, n_chunks, ctx_len):
    tq = q_ref.shape[0]
    qt_ref[...] = q_ref[...].T
    m_ref[...] = jnp.full(m_ref.shape, -jnp.inf, F32)
    acc_ref[...] = jnp.zeros(acc_ref.shape, F32)

    hcols = lambda h: slice(h * HEAD_PAD, (h + 1) * HEAD_PAD)
    vrows = lambda h: slice(h * MLA_V, (h + 1) * MLA_V)
    ksub = lambda s: slice(s * MLA_KSUB, (s + 1) * MLA_KSUB)
    qcols = lambda qh: slice(qh * MLA_QSUB, (qh + 1) * MLA_QSUB)

    def run_chunk(k_of, vt_of, n_sub):
        units = [(qh, h, s) for qh in range(tq // MLA_QSUB) for h in range(MLA_HEADS)
                 for s in range(n_sub)]

        def scores(u):
            qh, h, s = u
            return _softmax_scores(k_of(h, s), qt_ref[hcols(h), qcols(qh)])

        def load(u):
            qh, h, _ = u
            return (m_ref[h, :, qcols(qh)], acc_ref[h, :, qcols(qh)])

        def store(u, state):
            qh, h, _ = u
            m_ref[h, :, qcols(qh)] = state[0]
            acc_ref[h, :, qcols(qh)] = state[1]

        _softmax_pipeline(units, scores, lambda u: vt_of(u[1], u[2]), load, store,
                          is_first=lambda u: u[2] == 0, is_last=lambda u: u[2] == n_sub - 1)

    run_chunk(lambda h, s: kc_ref[ksub(s), hcols(h)],
              lambda h, s: vtc_ref[0, vrows(h), ksub(s)],
              ctx_len // MLA_KSUB)

    spt = ROW_TILE // MLA_KSUB

    def body(c, carry):
        r0 = pl.multiple_of(c * (MLA_TILES_PER_ITER * ROW_TILE), ROW_TILE)
        run_chunk(lambda h, s: k_ref[pl.ds(r0 + s * MLA_KSUB, MLA_KSUB), hcols(h)],
                  lambda h, s: vt_ref[c * MLA_TILES_PER_ITER + s // spt, vrows(h), ksub(s % spt)],
                  MLA_TILES_PER_ITER * spt)
        return carry

    lax.fori_loop(0, n_chunks // MLA_TILES_PER_ITER, body, 0)

    for h in range(MLA_HEADS):
        ot_ref[vrows(h), :] = acc_ref[h, :MLA_V, :] / acc_ref[h, MLA_V:MLA_V + 1, :]
    o_ref[...] = ot_ref[...].T.astype(BF16)


def _mla_latent(qm, km, vmt, *, n_batch, seq_len, ctx_len):
    tq = MLA_TQ
    nq = seq_len // tq
    tps = seq_len // ROW_TILE
    n_lat = n_batch * seq_len
    hw = MLA_HEADS * HEAD_PAD
    vw = MLA_HEADS * MLA_V
    ctx_blk0 = n_lat // ctx_len
    ctx_tile = n_lat // ROW_TILE
    cpt = ROW_TILE // ctx_len
    once = pl.Buffered(1)
    return pl.pallas_call(
        functools.partial(_mla_kernel, n_chunks=tps, ctx_len=ctx_len),
        grid=(n_batch, nq),
        in_specs=[
            pl.BlockSpec((tq, hw), lambda b, i: (b * nq + i, 0)),
            pl.BlockSpec((seq_len, hw), lambda b, i: (b, 0), pipeline_mode=once),
            pl.BlockSpec((tps, vw, ROW_TILE), lambda b, i: (b, 0, 0), pipeline_mode=once),
            pl.BlockSpec((ctx_len, hw), lambda b, i: (ctx_blk0 + b, 0)),
            pl.BlockSpec((1, vw, ctx_len), lambda b, i: (ctx_tile + b // cpt, 0, b % cpt)),
        ],
        out_specs=pl.BlockSpec((tq, vw), lambda b, i: (b * nq + i, 0)),
        out_shape=jax.ShapeDtypeStruct((qm.shape[0], vw), BF16),
        scratch_shapes=[pltpu.VMEM((hw, tq), BF16),
                        pltpu.VMEM((MLA_HEADS, 8, tq), F32),
                        pltpu.VMEM((MLA_HEADS, MLA_V + DENOM_ROWS, tq), F32),
                        pltpu.VMEM((vw, tq), F32)],
        compiler_params=pltpu.CompilerParams(
            dimension_semantics=("arbitrary", "arbitrary"), vmem_limit_bytes=VMEM_LIMIT),
        name="mla_latent",
    )(qm, km, vmt, km, vmt)


def _outffn_kernel(x_ref, yp_ref, ys_ref, ym_ref, yg_ref, mod_ref, wout_ref, g1_ref, b1_ref,
                   w1_ref, w2_ref, g2_ref, b2_ref, o_ref, *, alpha):
    q = wout_ref.shape[0] // 4
    gate1 = mod_ref[0, 2:3, :]
    shift = mod_ref[0, 3:4, :]
    scale = mod_ref[0, 4:5, :]
    gate2 = mod_ref[0, 5:6, :]
    dff = w1_ref.shape[1]
    tm = x_ref.shape[0]
    groups = [slice(g * (tm // FFN_ROW_GROUPS), (g + 1) * (tm // FFN_ROW_GROUPS))
              for g in range(FFN_ROW_GROUPS)]
    ys = [(_dot(yp_ref[r, :], wout_ref[0:q, :]) + _dot(ys_ref[r, :], wout_ref[q:2 * q, :])
           + _dot(ym_ref[r, :], wout_ref[2 * q:3 * q, :]) + _dot(yg_ref[r, :], wout_ref[3 * q:, :]))
          for r in groups]
    x1s = [_ln(alpha * x_ref[r, :] + gate1 * y) * g1_ref[...] + b1_ref[...]
           for r, y in zip(groups, ys)]
    for r, x1 in zip(groups, x1s):
        h = (_ln(x1) * (1.0 + scale) + shift).astype(BF16)
        f = None
        for c in range(dff // FF_CHUNK):
            a = jnp.maximum(_dot(h, w1_ref[:, c * FF_CHUNK:(c + 1) * FF_CHUNK]), 0.0)
            t = _dot((a * a).astype(BF16), w2_ref[c * FF_CHUNK:(c + 1) * FF_CHUNK, :])
            f = t if f is None else f + t
        o_ref[r, :] = _ln(alpha * x1 + gate2 * f) * g2_ref[...] + b2_ref[...]


def _outffn(xs, ys, mods, lw, *, n_tiles, n_lat_tiles, tiles_per_seq, alpha):
    n_all, D = xs.shape
    tm = ROW_TILE
    n_batch = n_lat_tiles // tiles_per_seq

    def mod_map(i):
        return (jnp.where(i < n_lat_tiles, i // tiles_per_seq, n_batch), 0, 0)

    row = lambda w: pl.BlockSpec((tm, w), lambda i: (i, 0))
    const = lambda a: pl.BlockSpec(a.shape, lambda i: (0,) * a.ndim,
                                   pipeline_mode=pl.Buffered(1))
    consts1 = [lw["w_out"], lw["ln1_g"], lw["ln1_b"], lw["w_ff1"], lw["w_ff2"],
               lw["ln2_g"], lw["ln2_b"]]
    return pl.pallas_call(
        functools.partial(_outffn_kernel, alpha=alpha),
        grid=(n_tiles,),
        in_specs=[row(D)] + [row(256)] * 4 + [pl.BlockSpec((1, 8, D), mod_map)]
                 + [const(a) for a in consts1],
        out_specs=row(D),
        out_shape=jax.ShapeDtypeStruct((n_tiles * tm, D), F32),
        compiler_params=pltpu.CompilerParams(
            dimension_semantics=("arbitrary",), vmem_limit_bytes=VMEM_LIMIT),
        name="outproj_ffn",
    )(xs, *ys, mods, *consts1)


def _swap_halves(w, n_axes_groups, half):
    shp = w.shape
    w = w.reshape(shp[:-1] + (n_axes_groups, 2, half))
    w = jnp.stack([-w[..., 1, :], w[..., 0, :]], axis=-2)
    return w.reshape(shp)


def _prepare_weights(w_in, pool_w, pool_scale, mla_q_norm, mla_w_uq, mla_kv_norm, mla_w_ukv,
                     sgu_norm_g, sgu_norm_b, sgu_w, sgu_b, w_out, ln1_g, ln1_b, w_ff1, w_ff2,
                     ln2_g, ln2_b):
    L, D, _ = w_in.shape
    c = 0
    w_pool = w_in[..., c:c + 256]; c += 256
    w_q = w_in[..., c:c + 256] * (SWA_HEAD_DIM ** -0.5 * LOG2E); c += 256
    w_cq = w_in[..., c:c + 256]; c += 256
    w_sgu = w_in[..., c:c + 512]; c += 512
    w_k = w_in[..., c:c + 128]; c += 128
    w_v = w_in[..., c:c + 128]; c += 128
    w_ckv = w_in[..., c:c + 128]; c += 128
    w_kr = w_in[..., c:c + 32]
    rot = SWA_HEAD_DIM // 4
    w_qsw = _swap_halves(w_q, SWA_HEADS * 2, rot)
    w_ksw = _swap_halves(w_k, SWA_KV_HEADS * 2, rot)
    w_krsw = _swap_halves(w_kr, 2, MLA_ROPE // 4)

    def kr_pad(w):
        return jnp.pad(w, [(0, 0), (0, 0), (MLA_NOPE, HEAD_PAD - MLA_NOPE - MLA_ROPE)])

    w_in_ext = jnp.concatenate([
        w_pool, w_q, w_qsw, w_cq, w_sgu, w_k, w_ksw, w_v, w_ckv, kr_pad(w_kr), kr_pad(w_krsw),
    ], axis=-1).astype(BF16)
    assert w_in_ext.shape[-1] == D_IN_EXT

    qd = MLA_NOPE + MLA_ROPE
    wq = mla_w_uq.reshape(L, MLA_Q_RANK, MLA_HEADS, qd) * (qd ** -0.5 * LOG2E)
    wq_rope_sw = _swap_halves(wq[..., MLA_NOPE:], 2, MLA_ROPE // 4)
    pad_q = [(0, 0), (0, 0), (0, 0), (0, HEAD_PAD - qd)]
    wq_full = jnp.pad(wq, pad_q).reshape(L, MLA_Q_RANK, MLA_HEADS * HEAD_PAD)
    wq_sw = jnp.pad(wq_rope_sw, [(0, 0), (0, 0), (0, 0), (MLA_NOPE, HEAD_PAD - qd)])
    wq_sw = wq_sw.reshape(L, MLA_Q_RANK, MLA_HEADS * HEAD_PAD)
    w_uq_ext = jnp.concatenate([wq_full, wq_sw], axis=-1).astype(BF16)

    wkv = mla_w_ukv.reshape(L, MLA_KV_RANK, MLA_HEADS, MLA_NOPE + MLA_V)
    w_kn = jnp.pad(wkv[..., :MLA_NOPE], [(0, 0), (0, 0), (0, 0), (0, HEAD_PAD - MLA_NOPE)])
    w_kn = w_kn.reshape(L, MLA_KV_RANK, MLA_HEADS * HEAD_PAD).astype(BF16)
    w_vt = wkv[..., MLA_NOPE:].reshape(L, MLA_KV_RANK, MLA_HEADS * MLA_V)
    w_vt = jnp.swapaxes(w_vt, 1, 2).astype(BF16)

    g_n = len(POOL_WINDOWS)
    eye = jnp.eye(g_n, dtype=F32)
    pool_bd = (pool_w[:, :, :, None, :] * eye[None, :, None, :, None])
    pool_bd = pool_bd.reshape(L, g_n * POOL_GROUP, g_n * POOL_GROUP).astype(BF16)

    sgu_bias = jnp.repeat(jnp.swapaxes(sgu_b, 1, 2), SGU_HEAD_DIM, axis=-1)

    r2 = lambda a: a.reshape(L, 1, a.shape[-1])
    return dict(
        w_in=w_in_ext, q_norm=r2(mla_q_norm), w_uq=w_uq_ext, kv_norm=r2(mla_kv_norm),
        w_kn=w_kn, w_vt=w_vt, sgu_g=r2(sgu_norm_g), sgu_b=r2(sgu_norm_b),
        sgu_w=sgu_w.astype(BF16), sgu_bias=sgu_bias,
        pool_w=pool_bd, pool_scale=r2(pool_scale),
        w_out=w_out.astype(BF16), ln1_g=r2(ln1_g), ln1_b=r2(ln1_b),
        w_ff1=w_ff1.astype(BF16), w_ff2=w_ff2.astype(BF16), ln2_g=r2(ln2_g), ln2_b=r2(ln2_b),
    )


def _rope_tables(n):
    pos = np.arange(n)
    row = (pos // GRID_W).astype(np.float32)
    col = (pos % GRID_W).astype(np.float32)

    def head_table(d_rot, lead, width):
        d_ax = d_rot // 2
        inv = ROPE_BASE ** (-jnp.arange(0, d_ax, 2, dtype=F32) / d_ax)
        ar = jnp.asarray(row)[:, None] * inv
        ac = jnp.asarray(col)[:, None] * inv
        cos = jnp.concatenate([jnp.cos(ar), jnp.cos(ar), jnp.cos(ac), jnp.cos(ac)], axis=-1)
        sin = jnp.concatenate([jnp.sin(ar), jnp.sin(ar), jnp.sin(ac), jnp.sin(ac)], axis=-1)
        tail = width - lead - d_rot
        cos = jnp.pad(cos, [(0, 0), (lead, tail)], constant_values=1.0)
        sin = jnp.pad(sin, [(0, 0), (lead, tail)])
        ident = (jnp.ones((ROW_TILE, width), F32), jnp.zeros((ROW_TILE, width), F32))
        return jnp.concatenate([cos, ident[0]], axis=0), jnp.concatenate([sin, ident[1]], axis=0)

    cos_h, sin_h = head_table(SWA_HEAD_DIM, 0, SWA_HEAD_DIM)
    cos_h = jnp.concatenate([cos_h] * SWA_KV_HEADS, axis=-1)
    sin_h = jnp.concatenate([sin_h] * SWA_KV_HEADS, axis=-1)
    cos_m, sin_m = head_table(MLA_ROPE, MLA_NOPE, HEAD_PAD)
    return cos_h, sin_h, cos_m, sin_m


def kernel(x, c, ctx, c_ctx, w_ada, b_ada, w_in, pool_w, pool_scale, swa_sink, mla_q_norm,
           mla_w_uq, mla_kv_norm, mla_w_ukv, sgu_norm_g, sgu_norm_b, sgu_w, sgu_b, w_out,
           ln1_g, ln1_b, w_ff1, w_ff2, ln2_g, ln2_b):
    B, n, D = x.shape
    C = ctx.shape[1]
    L = w_in.shape[0]
    assert n % (MLA_TILES_PER_ITER * ROW_TILE) == 0 and MIX_ROWS == ROW_TILE and n % MLA_TQ == 0
    assert MLA_TQ % MLA_QSUB == 0 and ROW_TILE % MLA_KSUB == 0 and C % MLA_KSUB == 0
    assert (B * C) % ROW_TILE == 0 and ROW_TILE % C == 0 and C % BLOCK == 0 and B + 1 <= 8
    alpha = (2 * L) ** 0.25
    n_lat = B * n
    n_lat_tiles = n_lat // ROW_TILE
    n_all_tiles = (n_lat + B * C) // ROW_TILE
    tiles_per_seq = n // ROW_TILE

    cond = jnp.concatenate([c, c_ctx[None, :], jnp.zeros((8 - B - 1, D), F32)], axis=0)
    mods_all = _modulation(cond, w_ada, b_ada)
    mods_all = jnp.pad(mods_all.reshape(L, 8, N_MOD, D),
                       [(0, 0), (0, 0), (0, 8 - N_MOD), (0, 0)])

    W = _prepare_weights(w_in, pool_w, pool_scale, mla_q_norm, mla_w_uq, mla_kv_norm, mla_w_ukv,
                         sgu_norm_g, sgu_norm_b, sgu_w, sgu_b, w_out, ln1_g, ln1_b, w_ff1,
                         w_ff2, ln2_g, ln2_b)
    tabs = _rope_tables(n)

    xs = jnp.concatenate([x.reshape(n_lat, D), ctx.reshape(B * C, D)], axis=0)
    for l in range(L):
        last = l == L - 1
        lw = {k: v[l] for k, v in W.items()}
        mods = mods_all[l]
        sink = swa_sink[l]
        zpool, qs, ks, vs, vst, qm, km, vmt, ysgu = _inproj(
            xs, mods, tabs, lw, n_lat_tiles=n_lat_tiles, tiles_per_seq=tiles_per_seq)
        ypool, yswa = _local_mixers(sink, zpool, qs, ks, vst, lw,
                                    n_batch=B, seq_len=n, ctx_len=C)
        ymla = _mla_latent(qm, km, vmt, n_batch=B, seq_len=n, ctx_len=C)
        if last:
            n_tiles = n_lat_tiles
        else:
            ypool, yswa, ymla = _ctx_mixers(sink, zpool, qs, ks, vs, qm, km, vmt, lw,
                                            (ypool, yswa, ymla), n_batch=B, seq_len=n, ctx_len=C)
            n_tiles = n_all_tiles
        xs = _outffn(xs, (ypool, yswa, ymla, ysgu), mods, lw, n_tiles=n_tiles,
                     n_lat_tiles=n_lat_tiles, tiles_per_seq=tiles_per_seq, alpha=alpha)
    return xs[:n_lat].reshape(B, n, D)
```

```python
import functools
import math

import jax
import jax.numpy as jnp
import numpy as np
from jax import lax
from jax.experimental import pallas as pl
from jax.experimental.pallas import tpu as pltpu

F32 = jnp.float32
BF16 = jnp.bfloat16

GRID_W = 64
POOL_WINDOWS = (2, 4, 8, 16)
POOL_GROUP = 64
D_POOL = 256
SWA_HEADS = 4
SWA_KV_HEADS = 2
SWA_HEAD_DIM = 64
SWA_WINDOW = 128
BLOCK = 128
MLA_HEADS = 4
MLA_NOPE = 64
MLA_ROPE = 32
MLA_V = 64
MLA_Q_RANK = 256
MLA_KV_RANK = 128
SGU_HEADS = 4
SGU_WIDTH = 256
SGU_HEAD_DIM = 64
SGU_CHUNK = 128
ROPE_BASE = 10000.0
N_MOD = 6
EPS = 1e-6

LANE = 128
ROW_TILE = 512
MIX_ROWS = 512
HALO = 8
MLA_TQ = 1024
MLA_QSUB = 256
MLA_KSUB = 256
DENOM_ROWS = 16
MLA_TILES_PER_ITER = 4
SOFTMAX_AHEAD = 8
LOG2E = math.log2(math.e)
FF_CHUNK = 1024
INPROJ_ROW_GROUPS = 1
FFN_ROW_GROUPS = 2
VMEM_LIMIT = 56 * 1024 * 1024

HEAD_PAD = LANE

G_POOL = (0, 256)
G_Q = (256, 256)
G_QSW = (512, 256)
G_CQ = (768, 256)
G_SGU = (1024, 512)
G_K = (1536, 128)
G_KSW = (1664, 128)
G_V = (1792, 128)
G_CKV = (1920, 128)
G_KR = (2048, 128)
G_KRSW = (2176, 128)
D_IN_EXT = 2304


def _ln(x):
    mu = jnp.mean(x, axis=-1, keepdims=True)
    xc = x - mu
    var = jnp.mean(xc * xc, axis=-1, keepdims=True)
    return xc * lax.rsqrt(var + EPS)


def _dot(a, b):
    return jnp.dot(a, b, preferred_element_type=F32)


def _dot_nt(a, b):
    return lax.dot_general(a, b, (((1,), (1,)), ((), ())), preferred_element_type=F32)


def _gelu(x):
    return 0.5 * x * (1.0 + lax.erf(x * (1.0 / math.sqrt(2.0))))


def _mod_kernel(c_ref, w_ref, b_ref, o_ref):
    c = c_ref[...]
    s = c * (1.0 / (1.0 + jnp.exp(-c)))
    o_ref[0] = jnp.dot(s, w_ref[0], preferred_element_type=F32,
                       precision=lax.Precision.HIGHEST) + b_ref[0]


def _modulation(cond, w_ada, b_ada):
    L, D, D6 = w_ada.shape
    nc = D6 // D
    return pl.pallas_call(
        _mod_kernel,
        grid=(L, nc),
        in_specs=[
            pl.BlockSpec((8, D), lambda l, j: (0, 0)),
            pl.BlockSpec((1, D, D), lambda l, j: (l, 0, j)),
            pl.BlockSpec((1, 1, D), lambda l, j: (l, 0, j)),
        ],
        out_specs=pl.BlockSpec((1, 8, D), lambda l, j: (l, 0, j)),
        out_shape=jax.ShapeDtypeStruct((L, 8, D6), F32),
        compiler_params=pltpu.CompilerParams(
            dimension_semantics=("arbitrary", "arbitrary"), vmem_limit_bytes=VMEM_LIMIT),
        name="adaln_modulation",
    )(cond, w_ada, b_ada.reshape(L, 1, D6))


def _inproj_kernel(x_ref, mod_ref, cq_ref, sq_ref, cm_ref, sm_ref, win_ref,
                   qg_ref, wuq_ref, kvg_ref, wkn_ref, wvt_ref,
                   sg_ref, sb_ref, sw_ref, sbias_ref,
                   zpool_ref, qs_ref, ks_ref, vs_ref, vst_ref, qm_ref, km_ref, vmt_ref, ysgu_ref):
    shift = mod_ref[0, 0:1, :]
    scale = mod_ref[0, 1:2, :]
    tm = x_ref.shape[0]
    gr = tm // INPROJ_ROW_GROUPS
    for grp_i in range(INPROJ_ROW_GROUPS):
        r = slice(grp_i * gr, (grp_i + 1) * gr)
        h = (_ln(x_ref[r, :]) * (1.0 + scale) + shift).astype(BF16)

        def proj(g, h=h):
            return _dot(h, win_ref[:, g[0]:g[0] + g[1]])

        zpool_ref[r, :] = proj(G_POOL)

        cos_k = cq_ref[r, :]
        sin_k = sq_ref[r, :]
        reps = SWA_HEADS // SWA_KV_HEADS
        cos_q = jnp.concatenate([cos_k] * reps, axis=-1)
        sin_q = jnp.concatenate([sin_k] * reps, axis=-1)
        qs_ref[r, :] = (proj(G_Q) * cos_q + proj(G_QSW) * sin_q).astype(BF16)
        ks_ref[r, :] = (proj(G_K) * cos_k + proj(G_KSW) * sin_k).astype(BF16)
        v_swa = proj(G_V)
        vs_ref[r, :] = v_swa.astype(BF16)
        vst_ref[0, :, r] = v_swa.T.astype(BF16)

        cos_m = cm_ref[r, :]
        sin_m = sm_ref[r, :]
        cq = proj(G_CQ)
        cqn = cq * lax.rsqrt(jnp.mean(cq * cq, axis=-1, keepdims=True) + EPS) * qg_ref[...]
        q2 = _dot(cqn.astype(BF16), wuq_ref[...])
        hw = MLA_HEADS * HEAD_PAD
        cos_m4 = jnp.concatenate([cos_m] * MLA_HEADS, axis=-1)
        sin_m4 = jnp.concatenate([sin_m] * MLA_HEADS, axis=-1)
        qm_ref[r, :] = (q2[:, :hw] * cos_m4 + q2[:, hw:] * sin_m4).astype(BF16)

        ckv = proj(G_CKV)
        ckvn = (ckv * lax.rsqrt(jnp.mean(ckv * ckv, axis=-1, keepdims=True) + EPS)
                * kvg_ref[...]).astype(BF16)
        kr = proj(G_KR) * cos_m + proj(G_KRSW) * sin_m
        kn = _dot(ckvn, wkn_ref[...])
        km_ref[r, :] = (kn + jnp.concatenate([kr] * MLA_HEADS, axis=-1)).astype(BF16)
        vmt_ref[0, :, r] = _dot_nt(wvt_ref[...], ckvn).astype(BF16)

        z = _gelu(proj(G_SGU))
        u = z[:, :SGU_WIDTH]
        v = _ln(z[:, SGU_WIDTH:]) * sg_ref[...] + sb_ref[...]
        vb = v.astype(BF16)
        for c in range(gr // SGU_CHUNK):
            r0 = c * SGU_CHUNK
            cols = []
            for hh in range(SGU_HEADS):
                c0 = hh * SGU_HEAD_DIM
                cols.append(_dot(sw_ref[hh], vb[r0:r0 + SGU_CHUNK, c0:c0 + SGU_HEAD_DIM]))
            mixed = jnp.concatenate(cols, axis=-1) + sbias_ref[...]
            ysgu_ref[grp_i * gr + r0:grp_i * gr + r0 + SGU_CHUNK, :] = (
                u[r0:r0 + SGU_CHUNK, :] * mixed).astype(BF16)


def _inproj(xs, mods, tabs, lw, *, n_lat_tiles, tiles_per_seq):
    n_all, D = xs.shape
    tm = ROW_TILE
    nt = n_all // tm
    n_batch = n_lat_tiles // tiles_per_seq

    def mod_map(i):
        return (jnp.where(i < n_lat_tiles, i // tiles_per_seq, n_batch), 0, 0)

    def pos_map(i):
        return (jnp.where(i < n_lat_tiles, i % tiles_per_seq, tiles_per_seq), 0)

    row = lambda w: pl.BlockSpec((tm, w), lambda i: (i, 0))
    full = lambda a: pl.BlockSpec(a.shape, lambda i: (0,) * a.ndim)
    cos_q, sin_q, cos_m, sin_m = tabs
    consts = [lw["w_in"], lw["q_norm"], lw["w_uq"], lw["kv_norm"], lw["w_kn"], lw["w_vt"],
              lw["sgu_g"], lw["sgu_b"], lw["sgu_w"], lw["sgu_bias"]]
    hw = MLA_HEADS * HEAD_PAD
    out_shape = [
        jax.ShapeDtypeStruct((n_all, D_POOL), F32),
        jax.ShapeDtypeStruct((n_all, SWA_HEADS * SWA_HEAD_DIM), BF16),
        jax.ShapeDtypeStruct((n_all, SWA_KV_HEADS * SWA_HEAD_DIM), BF16),
        jax.ShapeDtypeStruct((n_all, SWA_KV_HEADS * SWA_HEAD_DIM), BF16),
        jax.ShapeDtypeStruct((nt, SWA_KV_HEADS * SWA_HEAD_DIM, tm), BF16),
        jax.ShapeDtypeStruct((n_all, hw), BF16),
        jax.ShapeDtypeStruct((n_all, hw), BF16),
        jax.ShapeDtypeStruct((nt, MLA_HEADS * MLA_V, tm), BF16),
        jax.ShapeDtypeStruct((n_all, SGU_WIDTH), BF16),
    ]
    out_specs = [row(D_POOL), row(SWA_HEADS * SWA_HEAD_DIM), row(SWA_KV_HEADS * SWA_HEAD_DIM),
                 row(SWA_KV_HEADS * SWA_HEAD_DIM),
                 pl.BlockSpec((1, SWA_KV_HEADS * SWA_HEAD_DIM, tm), lambda i: (i, 0, 0)),
                 row(hw), row(hw),
                 pl.BlockSpec((1, MLA_HEADS * MLA_V, tm), lambda i: (i, 0, 0)),
                 row(SGU_WIDTH)]
    return pl.pallas_call(
        _inproj_kernel,
        grid=(nt,),
        in_specs=[row(D),
                  pl.BlockSpec((1, 8, D), mod_map),
                  pl.BlockSpec((tm, cos_q.shape[1]), pos_map),
                  pl.BlockSpec((tm, sin_q.shape[1]), pos_map),
                  pl.BlockSpec((tm, cos_m.shape[1]), pos_map),
                  pl.BlockSpec((tm, sin_m.shape[1]), pos_map)] + [full(a) for a in consts],
        out_specs=out_specs,
        out_shape=out_shape,
        compiler_params=pltpu.CompilerParams(
            dimension_semantics=("arbitrary",), vmem_limit_bytes=VMEM_LIMIT),
        name="inproj",
    )(xs, mods, cos_q, sin_q, cos_m, sin_m, *consts)


def _pool_from_scratch(xe_ref, rows, t0, seq_len, pw_ref, ps_ref):
    def win(d):
        return xe_ref[HALO + d:HALO + d + rows, :]

    x = win(0)
    t = t0 + lax.broadcasted_iota(jnp.int32, (rows, 1), 0)
    lane_group = lax.broadcasted_iota(jnp.int32, (1, D_POOL), 1) // POOL_GROUP
    s = x
    lo_d, hi_d = 0, 0
    mean = jnp.zeros_like(x)
    for gi, w in enumerate(POOL_WINDOWS):
        for d in range(-(w // 2), lo_d):
            s = s + win(d)
        for d in range(hi_d + 1, w // 2):
            s = s + win(d)
        lo_d, hi_d = -(w // 2), w // 2 - 1
        cnt = (jnp.minimum(t + w // 2, seq_len) - jnp.maximum(t - w // 2, 0)).astype(F32)
        mean = jnp.where(lane_group == gi, s / cnt, mean)
    d = (mean - x).astype(BF16)
    return _dot(d, pw_ref[...]) * ps_ref[...]


def _sink_softmax_pv(s_list, v_list, sink2):
    m = sink2
    for s in s_list:
        m = jnp.maximum(m, jnp.max(s, axis=-1, keepdims=True))
    denom = jnp.exp2(sink2 - m)
    o = None
    for s, v in zip(s_list, v_list):
        p = jnp.exp2(s - m)
        denom = denom + jnp.sum(p, axis=-1, keepdims=True)
        t = _dot(p.astype(BF16), v)
        o = t if o is None else o + t
    return o / denom


def _sublane_allmax(x):
    for shift in (4, 2, 1):
        x = jnp.maximum(x, pltpu.roll(x, shift, axis=0))
    return x


def _softmax_scores(k_sub, qt):
    return _dot(k_sub, qt).reshape(k_sub.shape[0] // 8, 8, qt.shape[1])


def _softmax_unit(s3, vt_sub, m, acc):
    ks = s3.shape[0] * 8
    qs = s3.shape[2]
    nv = acc.shape[0]
    m_new = jnp.maximum(m, _sublane_allmax(jnp.max(s3, axis=0)))
    alpha = jnp.exp2(m - m_new)
    p = jnp.exp2(s3 - m_new[None]).reshape(ks, qs).astype(BF16)
    vt_ext = jnp.concatenate([vt_sub, jnp.ones((DENOM_ROWS, ks), BF16)], axis=0)
    acc = (acc.reshape(nv // 8, 8, qs) * alpha[None]).reshape(nv, qs) + _dot(vt_ext, p)
    return m_new, acc


def _softmax_pipeline(units, score_of, value_of, load, store, *, is_first, is_last):
    pending = [score_of(u) for u in units[:SOFTMAX_AHEAD]]
    state = None
    for i, u in enumerate(units):
        if i + SOFTMAX_AHEAD < len(units):
            pending.append(score_of(units[i + SOFTMAX_AHEAD]))
        if is_first(u):
            state = load(u)
        state = _softmax_unit(pending.pop(0), value_of(u), *state)
        if is_last(u):
            store(u, state)


def _split_v(v):
    lane = lax.broadcasted_iota(jnp.int32, v.shape, 1)
    zero = jnp.zeros_like(v)
    lo0 = jnp.where(lane < SWA_HEAD_DIM, v, zero)
    hi1 = jnp.where(lane >= SWA_HEAD_DIM, v, zero)
    swapped = jnp.concatenate([v[:, SWA_HEAD_DIM:], v[:, :SWA_HEAD_DIM]], axis=-1)
    hi0 = jnp.where(lane >= SWA_HEAD_DIM, swapped, zero)
    lo1 = jnp.where(lane < SWA_HEAD_DIM, swapped, zero)
    return ((lo0, hi0), (lo1, hi1))


def _local_kernel(sink_ref, zp_ref, zpp_ref, zpn_ref, q_ref, k_ref, kp_ref, kn_ref,
                  vt_ref, vtp_ref, vtn_ref, kc_ref, vtc_ref, pw_ref, ps_ref, ypool_in, yswa_in,
                  ypool_ref, yswa_ref, xe_ref, qt_ref, ke_ref, vte_ref, ot_ref, *, seq_len):
    del ypool_in, yswa_in
    j = pl.program_id(1)
    rows = MIX_ROWS
    t0 = j * rows
    first = j == 0
    last = j == pl.num_programs(1) - 1

    xe_ref[0:HALO, :] = jnp.where(first, 0.0, zpp_ref[...])
    xe_ref[HALO:HALO + rows, :] = zp_ref[...]
    xe_ref[HALO + rows:, :] = jnp.where(last, 0.0, zpn_ref[...])
    ypool_ref[...] = _pool_from_scratch(xe_ref, rows, t0, seq_len, pw_ref, ps_ref).astype(BF16)

    qt_ref[...] = q_ref[...].T
    ke_ref[0:BLOCK, :] = kp_ref[...]
    ke_ref[BLOCK:BLOCK + rows, :] = k_ref[...]
    ke_ref[BLOCK + rows:, :] = kn_ref[...]
    vte_ref[:, 0:BLOCK] = vtp_ref[0]
    vte_ref[:, BLOCK:BLOCK + rows] = vt_ref[0]
    vte_ref[:, BLOCK + rows:] = vtn_ref[0]

    grp = SWA_HEADS // SWA_KV_HEADS
    qw = grp * BLOCK
    shape3 = (BLOCK // 8, 8, qw)
    key_j = (lax.broadcasted_iota(jnp.int32, shape3, 0) * 8
             + lax.broadcasted_iota(jnp.int32, shape3, 1))
    qry_i = lax.broadcasted_iota(jnp.int32, shape3, 2) % BLOCK
    rel = key_j - qry_i
    lane = lax.broadcasted_iota(jnp.int32, (8, qw), 1)
    nqb = rows // BLOCK
    n_ctx_sub = kc_ref.shape[0] // BLOCK

    kinds = ["cur", "prev", "next"] + ["ctx%d" % c for c in range(n_ctx_sub)]
    units = [(qb, g, kind) for qb in range(nqb) for g in range(SWA_KV_HEADS) for kind in kinds]

    def q_tile(qb, g):
        zero = jnp.zeros((SWA_HEAD_DIM, BLOCK), BF16)
        cols = []
        for e in range(grp):
            hq = g * grp + e
            qh = qt_ref[hq * SWA_HEAD_DIM:(hq + 1) * SWA_HEAD_DIM, qb * BLOCK:(qb + 1) * BLOCK]
            cols.append(jnp.concatenate([qh if gg == g else zero
                                         for gg in range(SWA_KV_HEADS)], axis=0))
        return jnp.concatenate(cols, axis=1)

    def scores(u):
        qb, g, kind = u
        if kind.startswith("ctx"):
            c = int(kind[3:])
            return _softmax_scores(kc_ref[c * BLOCK:(c + 1) * BLOCK, :], q_tile(qb, g))
        off = {"prev": 0, "cur": 1, "next": 2}[kind]
        r0 = (qb + off) * BLOCK
        s3 = _softmax_scores(ke_ref[r0:r0 + BLOCK, :], q_tile(qb, g))
        if kind == "prev":
            ok = rel >= 0
            if qb == 0:
                ok = ok & jnp.logical_not(first)
            s3 = jnp.where(ok, s3, -jnp.inf)
        elif kind == "next":
            ok = rel <= 0
            if qb == nqb - 1:
                ok = ok & jnp.logical_not(last)
            s3 = jnp.where(ok, s3, -jnp.inf)
        return s3

    def values(u):
        qb, g, kind = u
        grows = slice(g * SWA_HEAD_DIM, (g + 1) * SWA_HEAD_DIM)
        if kind.startswith("ctx"):
            c = int(kind[3:])
            return vtc_ref[0, grows, c * BLOCK:(c + 1) * BLOCK]
        off = {"prev": 0, "cur": 1, "next": 2}[kind]
        return vte_ref[grows, (qb + off) * BLOCK:(qb + off + 1) * BLOCK]

    def load(u):
        return (jnp.full((8, qw), -jnp.inf, F32),
                jnp.zeros((SWA_HEAD_DIM + DENOM_ROWS, qw), F32))

    def store(u, state):
        qb, g, _ = u
        m, acc = state
        sink2 = jnp.where(lane < BLOCK, sink_ref[g * grp], sink_ref[g * grp + 1]) * LOG2E
        m_fin = jnp.maximum(m, sink2)
        a = jnp.exp2(m - m_fin)
        denom = acc[SWA_HEAD_DIM:SWA_HEAD_DIM + 1]
        l_tot = denom * a[0:1] + jnp.exp2(sink2 - m_fin)[0:1]
        o = acc[:SWA_HEAD_DIM] * (a[0:1] / l_tot)
        for e in range(grp):
            hq = g * grp + e
            ot_ref[hq * SWA_HEAD_DIM:(hq + 1) * SWA_HEAD_DIM, qb * BLOCK:(qb + 1) * BLOCK] = (
                o[:, e * BLOCK:(e + 1) * BLOCK])

    _softmax_pipeline(units, scores, values, load, store,
                      is_first=lambda u: u[2] == kinds[0], is_last=lambda u: u[2] == kinds[-1])
    yswa_ref[...] = ot_ref[...].T.astype(BF16)


def _local_mixers(sink, zpool, qs, ks, vst, lw, *, n_batch, seq_len, ctx_len):
    rows = MIX_ROWS
    spb = seq_len // rows
    n_lat = n_batch * seq_len
    ctx_blk0 = n_lat // ctx_len
    ctx_tile = n_lat // ROW_TILE
    cpt = ROW_TILE // ctx_len
    bpt = ROW_TILE // BLOCK

    main = lambda w: pl.BlockSpec((rows, w), lambda b, j, *_: (b * spb + j, 0))

    def prev_blk(unit):
        per = rows // unit
        return lambda b, j: jnp.maximum((b * spb + j) * per - 1, 0)

    def next_blk(unit):
        per = rows // unit
        last_blk = n_lat // unit - 1
        return lambda b, j: jnp.minimum((b * spb + j + 1) * per, last_blk)

    def row_halo(unit, w, blk):
        return pl.BlockSpec((unit, w), lambda b, j, *_: (blk(unit)(b, j), 0))

    def lane_halo(blk):
        f = blk(BLOCK)
        return pl.BlockSpec((1, vw, BLOCK), lambda b, j, *_: (f(b, j) // bpt, 0, f(b, j) % bpt))

    full = lambda a: pl.BlockSpec(a.shape, lambda b, j, *_: (0,) * a.ndim)
    kw = SWA_KV_HEADS * SWA_HEAD_DIM
    vw = SWA_KV_HEADS * SWA_HEAD_DIM
    qw = SWA_HEADS * SWA_HEAD_DIM
    grid_spec = pltpu.PrefetchScalarGridSpec(
        num_scalar_prefetch=1,
        grid=(n_batch, spb),
        in_specs=[
            main(D_POOL),
            row_halo(HALO, D_POOL, prev_blk),
            row_halo(HALO, D_POOL, next_blk),
            main(qw),
            main(kw),
            row_halo(BLOCK, kw, prev_blk),
            row_halo(BLOCK, kw, next_blk),
            pl.BlockSpec((1, vw, rows), lambda b, j, *_: (b * spb + j, 0, 0)),
            lane_halo(prev_blk),
            lane_halo(next_blk),
            pl.BlockSpec((ctx_len, kw), lambda b, j, *_: (ctx_blk0 + b, 0)),
            pl.BlockSpec((1, vw, ctx_len), lambda b, j, *_: (ctx_tile + b // cpt, 0, b % cpt)),
            full(lw["pool_w"]), full(lw["pool_scale"]),
            pl.BlockSpec(memory_space=pl.ANY), pl.BlockSpec(memory_space=pl.ANY),
        ],
        out_specs=[main(D_POOL), main(SWA_HEADS * SWA_HEAD_DIM)],
        scratch_shapes=[
            pltpu.VMEM((rows + 2 * HALO, D_POOL), F32),
            pltpu.VMEM((qw, rows), BF16),
            pltpu.VMEM((rows + 2 * BLOCK, kw), BF16),
            pltpu.VMEM((vw, rows + 2 * BLOCK), BF16),
            pltpu.VMEM((SWA_HEADS * SWA_HEAD_DIM, rows), F32),
        ],
    )
    return pl.pallas_call(
        functools.partial(_local_kernel, seq_len=seq_len),
        grid_spec=grid_spec,
        out_shape=[jax.ShapeDtypeStruct((zpool.shape[0], D_POOL), BF16),
                   jax.ShapeDtypeStruct((zpool.shape[0], SWA_HEADS * SWA_HEAD_DIM), BF16)],
        input_output_aliases={15: 0, 16: 1},
        compiler_params=pltpu.CompilerParams(
            dimension_semantics=("arbitrary", "arbitrary"), vmem_limit_bytes=VMEM_LIMIT),
        name="local_mixers",
    )(sink, zpool, zpool, zpool, qs, ks, ks, ks, vst, vst, vst, ks, vst,
      lw["pool_w"], lw["pool_scale"],
      jnp.zeros((zpool.shape[0], D_POOL), BF16),
      jnp.zeros((zpool.shape[0], SWA_HEADS * SWA_HEAD_DIM), BF16))


def _ctx_kernel(sink_ref, zp_ref, q_ref, k_ref, v_ref, qm_ref, km_ref, vmt_ref, pw_ref, ps_ref,
                ypool_in, yswa_in, ymla_in, ypool_ref, yswa_ref, ymla_ref, xe_ref, *, ctx_len):
    del ypool_in, yswa_in, ymla_in
    rows = ctx_len
    xe_ref[0:HALO, :] = jnp.zeros((HALO, D_POOL), F32)
    xe_ref[HALO:HALO + rows, :] = zp_ref[...]
    xe_ref[HALO + rows:, :] = jnp.zeros((HALO, D_POOL), F32)
    ypool_ref[...] = _pool_from_scratch(xe_ref, rows, 0, ctx_len, pw_ref, ps_ref).astype(BF16)

    v_split = _split_v(v_ref[...])
    grp = SWA_HEADS // SWA_KV_HEADS
    for g in range(SWA_KV_HEADS):
        k = k_ref[:, g * SWA_HEAD_DIM:(g + 1) * SWA_HEAD_DIM]
        o = None
        for e in range(grp):
            hq = g * grp + e
            q = q_ref[:, hq * SWA_HEAD_DIM:(hq + 1) * SWA_HEAD_DIM]
            t = _sink_softmax_pv([_dot_nt(q, k)], [v_split[g][e]], sink_ref[hq] * LOG2E)
            o = t if o is None else o + t
        yswa_ref[:, g * LANE:(g + 1) * LANE] = o.astype(BF16)

    outs = []
    for hh in range(MLA_HEADS):
        q = qm_ref[:, hh * HEAD_PAD:(hh + 1) * HEAD_PAD]
        k = km_ref[:, hh * HEAD_PAD:(hh + 1) * HEAD_PAD]
        st = _dot_nt(k, q)
        m = jnp.max(st, axis=0, keepdims=True)
        p = jnp.exp2(st - m)
        l = jnp.sum(p, axis=0, keepdims=True)
        ot = _dot(vmt_ref[0, hh * MLA_V:(hh + 1) * MLA_V, :], p.astype(BF16))
        outs.append(ot / l)
    ymla_ref[...] = jnp.concatenate(outs, axis=0).T.astype(BF16)


def _ctx_mixers(sink, zpool, qs, ks, vs, qm, km, vmt, lw, ys, *, n_batch, seq_len, ctx_len):
    n_lat = n_batch * seq_len
    blk0 = n_lat // ctx_len
    ctx_tile = n_lat // ROW_TILE
    cpt = ROW_TILE // ctx_len
    rowb = lambda w: pl.BlockSpec((ctx_len, w), lambda b, *_: (blk0 + b, 0))
    outb = rowb
    full = lambda a: pl.BlockSpec(a.shape, lambda b, *_: (0,) * a.ndim)
    n_in = 10
    hw = MLA_HEADS * HEAD_PAD
    grid_spec = pltpu.PrefetchScalarGridSpec(
        num_scalar_prefetch=1,
        grid=(n_batch,),
        in_specs=[rowb(D_POOL), rowb(SWA_HEADS * SWA_HEAD_DIM), rowb(SWA_KV_HEADS * SWA_HEAD_DIM),
                  rowb(SWA_KV_HEADS * SWA_HEAD_DIM), rowb(hw), rowb(hw),
                  pl.BlockSpec((1, MLA_HEADS * MLA_V, ctx_len),
                               lambda b, *_: (ctx_tile + b // cpt, 0, b % cpt)),
                  full(lw["pool_w"]), full(lw["pool_scale"])]
                 + [pl.BlockSpec(memory_space=pl.ANY)] * len(ys),
        out_specs=[outb(D_POOL), outb(SWA_HEADS * SWA_HEAD_DIM), outb(MLA_HEADS * MLA_V)],
        scratch_shapes=[pltpu.VMEM((ctx_len + 2 * HALO, D_POOL), F32)],
    )
    return pl.pallas_call(
        functools.partial(_ctx_kernel, ctx_len=ctx_len),
        grid_spec=grid_spec,
        out_shape=[jax.ShapeDtypeStruct(y.shape, y.dtype) for y in ys],
        input_output_aliases={n_in + i: i for i in range(len(ys))},
        compiler_params=pltpu.CompilerParams(
            dimension_semantics=("arbitrary",), vmem_limit_bytes=VMEM_LIMIT),
        name="ctx_mixers",
    )(sink, zpool, qs, ks, vs, qm, km, vmt, lw["pool_w"], lw["pool_scale"], *ys)


def _mla_kernel(q_ref, k_ref, vt_ref, kc_ref, vtc_ref, o_in, o_ref, qt_ref, m_ref, acc_ref, ot_ref,
                *, n_chunks, ctx_len):
    del o_in
    tq = q_ref.shape[0]
    qt_ref[...] = q_ref[...].T
    m_ref[...] = jnp.full(m_ref.shape, -jnp.inf, F32)
    acc_ref[...] = jnp.zeros(acc_ref.shape, F32)

    hcols = lambda h: slice(h * HEAD_PAD, (h + 1) * HEAD_PAD)
    vrows = lambda h: slice(h * MLA_V, (h + 1) * MLA_V)
    ksub = lambda s: slice(s * MLA_KSUB, (s + 1) * MLA_KSUB)
    qcols = lambda qh: slice(qh * MLA_QSUB, (qh + 1) * MLA_QSUB)

    def run_chunk(k_of, vt_of, n_sub):
        units = [(qh, h, s) for qh in range(tq // MLA_QSUB) for h in range(MLA_HEADS)
                 for s in range(n_sub)]

        def scores(u):
            qh, h, s = u
            return _softmax_scores(k_of(h, s), qt_ref[hcols(h), qcols(qh)])

        def load(u):
            qh, h, _ = u
            return (m_ref[h, :, qcols(qh)], acc_ref[h, :, qcols(qh)])

        def store(u, state):
            qh, h, _ = u
            m_ref[h, :, qcols(qh)] = state[0]
            acc_ref[h, :, qcols(qh)] = state[1]

        _softmax_pipeline(units, scores, lambda u: vt_of(u[1], u[2]), load, store,
                          is_first=lambda u: u[2] == 0, is_last=lambda u: u[2] == n_sub - 1)

    run_chunk(lambda h, s: kc_ref[ksub(s), hcols(h)],
              lambda h, s: vtc_ref[0, vrows(h), ksub(s)],
              ctx_len // MLA_KSUB)

    spt = ROW_TILE // MLA_KSUB

    def body(c, carry):
        r0 = pl.multiple_of(c * (MLA_TILES_PER_ITER * ROW_TILE), ROW_TILE)
        run_chunk(lambda h, s: k_ref[pl.ds(r0 + s * MLA_KSUB, MLA_KSUB), hcols(h)],
                  lambda h, s: vt_ref[c * MLA_TILES_PER_ITER + s // spt, vrows(h), ksub(s % spt)],
                  MLA_TILES_PER_ITER * spt)
        return carry

    lax.fori_loop(0, n_chunks // MLA_TILES_PER_ITER, body, 0)

    for h in range(MLA_HEADS):
        ot_ref[vrows(h), :] = acc_ref[h, :MLA_V, :] / acc_ref[h, MLA_V:MLA_V + 1, :]
    o_ref[...] = ot_ref[...].T.astype(BF16)


def _mla_latent(qm, km, vmt, *, n_batch, seq_len, ctx_len):
    tq = MLA_TQ
    nq = seq_len // tq
    tps = seq_len // ROW_TILE
    n_lat = n_batch * seq_len
    hw = MLA_HEADS * HEAD_PAD
    vw = MLA_HEADS * MLA_V
    ctx_blk0 = n_lat // ctx_len
    ctx_tile = n_lat // ROW_TILE
    cpt = ROW_TILE // ctx_len
    once = pl.Buffered(1)
    return pl.pallas_call(
        functools.partial(_mla_kernel, n_chunks=tps, ctx_len=ctx_len),
        grid=(n_batch, nq),
        in_specs=[
            pl.BlockSpec((tq, hw), lambda b, i: (b * nq + i, 0)),
            pl.BlockSpec((seq_len, hw), lambda b, i: (b, 0), pipeline_mode=once),
            pl.BlockSpec((tps, vw, ROW_TILE), lambda b, i: (b, 0, 0), pipeline_mode=once),
            pl.BlockSpec((ctx_len, hw), lambda b, i: (ctx_blk0 + b, 0)),
            pl.BlockSpec((1, vw, ctx_len), lambda b, i: (ctx_tile + b // cpt, 0, b % cpt)),
            pl.BlockSpec(memory_space=pl.ANY),
        ],
        out_specs=pl.BlockSpec((tq, vw), lambda b, i: (b * nq + i, 0)),
        out_shape=jax.ShapeDtypeStruct((qm.shape[0], vw), BF16),
        input_output_aliases={5: 0},
        scratch_shapes=[pltpu.VMEM((hw, tq), BF16),
                        pltpu.VMEM((MLA_HEADS, 8, tq), F32),
                        pltpu.VMEM((MLA_HEADS, MLA_V + DENOM_ROWS, tq), F32),
                        pltpu.VMEM((vw, tq), F32)],
        compiler_params=pltpu.CompilerParams(
            dimension_semantics=("arbitrary", "arbitrary"), vmem_limit_bytes=VMEM_LIMIT),
        name="mla_latent",
    )(qm, km, vmt, km, vmt, jnp.zeros((qm.shape[0], vw), BF16))


def _outffn_kernel(x_ref, yp_ref, ys_ref, ym_ref, yg_ref, mod_ref, wout_ref, g1_ref, b1_ref,
                   w1_ref, w2_ref, g2_ref, b2_ref, o_ref, *, alpha):
    q = wout_ref.shape[0] // 4
    gate1 = mod_ref[0, 2:3, :]
    shift = mod_ref[0, 3:4, :]
    scale = mod_ref[0, 4:5, :]
    gate2 = mod_ref[0, 5:6, :]
    dff = w1_ref.shape[1]
    tm = x_ref.shape[0]
    groups = [slice(g * (tm // FFN_ROW_GROUPS), (g + 1) * (tm // FFN_ROW_GROUPS))
              for g in range(FFN_ROW_GROUPS)]
    ys = [(_dot(yp_ref[r, :], wout_ref[0:q, :]) + _dot(ys_ref[r, :], wout_ref[q:2 * q, :])
           + _dot(ym_ref[r, :], wout_ref[2 * q:3 * q, :]) + _dot(yg_ref[r, :], wout_ref[3 * q:, :]))
          for r in groups]
    x1s = [_ln(alpha * x_ref[r, :] + gate1 * y) * g1_ref[...] + b1_ref[...]
           for r, y in zip(groups, ys)]
    for r, x1 in zip(groups, x1s):
        h = (_ln(x1) * (1.0 + scale) + shift).astype(BF16)
        f = None
        for c in range(dff // FF_CHUNK):
            a = jnp.maximum(_dot(h, w1_ref[:, c * FF_CHUNK:(c + 1) * FF_CHUNK]), 0.0)
            t = _dot((a * a).astype(BF16), w2_ref[c * FF_CHUNK:(c + 1) * FF_CHUNK, :])
            f = t if f is None else f + t
        o_ref[r, :] = _ln(alpha * x1 + gate2 * f) * g2_ref[...] + b2_ref[...]


def _outffn(xs, ys, mods, lw, *, n_tiles, n_lat_tiles, tiles_per_seq, alpha):
    n_all, D = xs.shape
    tm = ROW_TILE
    n_batch = n_lat_tiles // tiles_per_seq

    def mod_map(i):
        return (jnp.where(i < n_lat_tiles, i // tiles_per_seq, n_batch), 0, 0)

    row = lambda w: pl.BlockSpec((tm, w), lambda i: (i, 0))
    const = lambda a: pl.BlockSpec(a.shape, lambda i: (0,) * a.ndim,
                                   pipeline_mode=pl.Buffered(1))
    consts1 = [lw["w_out"], lw["ln1_g"], lw["ln1_b"], lw["w_ff1"], lw["w_ff2"],
               lw["ln2_g"], lw["ln2_b"]]
    return pl.pallas_call(
        functools.partial(_outffn_kernel, alpha=alpha),
        grid=(n_tiles,),
        in_specs=[row(D)] + [row(256)] * 4 + [pl.BlockSpec((1, 8, D), mod_map)]
                 + [const(a) for a in consts1],
        out_specs=row(D),
        out_shape=jax.ShapeDtypeStruct((n_tiles * tm, D), F32),
        compiler_params=pltpu.CompilerParams(
            dimension_semantics=("arbitrary",), vmem_limit_bytes=VMEM_LIMIT),
        name="outproj_ffn",
    )(xs, *ys, mods, *consts1)


def _swap_halves(w, n_axes_groups, half):
    shp = w.shape
    w = w.reshape(shp[:-1] + (n_axes_groups, 2, half))
    w = jnp.stack([-w[..., 1, :], w[..., 0, :]], axis=-2)
    return w.reshape(shp)


def _prepare_weights(w_in, pool_w, pool_scale, mla_q_norm, mla_w_uq, mla_kv_norm, mla_w_ukv,
                     sgu_norm_g, sgu_norm_b, sgu_w, sgu_b, w_out, ln1_g, ln1_b, w_ff1, w_ff2,
                     ln2_g, ln2_b):
    L, D, _ = w_in.shape
    c = 0
    w_pool = w_in[..., c:c + 256]; c += 256
    w_q = w_in[..., c:c + 256] * (SWA_HEAD_DIM ** -0.5 * LOG2E); c += 256
    w_cq = w_in[..., c:c + 256]; c += 256
    w_sgu = w_in[..., c:c + 512]; c += 512
    w_k = w_in[..., c:c + 128]; c += 128
    w_v = w_in[..., c:c + 128]; c += 128
    w_ckv = w_in[..., c:c + 128]; c += 128
    w_kr = w_in[..., c:c + 32]
    rot = SWA_HEAD_DIM // 4
    w_qsw = _swap_halves(w_q, SWA_HEADS * 2, rot)
    w_ksw = _swap_halves(w_k, SWA_KV_HEADS * 2, rot)
    w_krsw = _swap_halves(w_kr, 2, MLA_ROPE // 4)

    def kr_pad(w):
        return jnp.pad(w, [(0, 0), (0, 0), (MLA_NOPE, HEAD_PAD - MLA_NOPE - MLA_ROPE)])

    w_in_ext = jnp.concatenate([
        w_pool, w_q, w_qsw, w_cq, w_sgu, w_k, w_ksw, w_v, w_ckv, kr_pad(w_kr), kr_pad(w_krsw),
    ], axis=-1).astype(BF16)
    assert w_in_ext.shape[-1] == D_IN_EXT

    qd = MLA_NOPE + MLA_ROPE
    wq = mla_w_uq.reshape(L, MLA_Q_RANK, MLA_HEADS, qd) * (qd ** -0.5 * LOG2E)
    wq_rope_sw = _swap_halves(wq[..., MLA_NOPE:], 2, MLA_ROPE // 4)
    pad_q = [(0, 0), (0, 0), (0, 0), (0, HEAD_PAD - qd)]
    wq_full = jnp.pad(wq, pad_q).reshape(L, MLA_Q_RANK, MLA_HEADS * HEAD_PAD)
    wq_sw = jnp.pad(wq_rope_sw, [(0, 0), (0, 0), (0, 0), (MLA_NOPE, HEAD_PAD - qd)])
    wq_sw = wq_sw.reshape(L, MLA_Q_RANK, MLA_HEADS * HEAD_PAD)
    w_uq_ext = jnp.concatenate([wq_full, wq_sw], axis=-1).astype(BF16)

    wkv = mla_w_ukv.reshape(L, MLA_KV_RANK, MLA_HEADS, MLA_NOPE + MLA_V)
    w_kn = jnp.pad(wkv[..., :MLA_NOPE], [(0, 0), (0, 0), (0, 0), (0, HEAD_PAD - MLA_NOPE)])
    w_kn = w_kn.reshape(L, MLA_KV_RANK, MLA_HEADS * HEAD_PAD).astype(BF16)
    w_vt = wkv[..., MLA_NOPE:].reshape(L, MLA_KV_RANK, MLA_HEADS * MLA_V)
    w_vt = jnp.swapaxes(w_vt, 1, 2).astype(BF16)

    g_n = len(POOL_WINDOWS)
    eye = jnp.eye(g_n, dtype=F32)
    pool_bd = (pool_w[:, :, :, None, :] * eye[None, :, None, :, None])
    pool_bd = pool_bd.reshape(L, g_n * POOL_GROUP, g_n * POOL_GROUP).astype(BF16)

    sgu_bias = jnp.repeat(jnp.swapaxes(sgu_b, 1, 2), SGU_HEAD_DIM, axis=-1)

    r2 = lambda a: a.reshape(L, 1, a.shape[-1])
    return dict(
        w_in=w_in_ext, q_norm=r2(mla_q_norm), w_uq=w_uq_ext, kv_norm=r2(mla_kv_norm),
        w_kn=w_kn, w_vt=w_vt, sgu_g=r2(sgu_norm_g), sgu_b=r2(sgu_norm_b),
        sgu_w=sgu_w.astype(BF16), sgu_bias=sgu_bias,
        pool_w=pool_bd, pool_scale=r2(pool_scale),
        w_out=w_out.astype(BF16), ln1_g=r2(ln1_g), ln1_b=r2(ln1_b),
        w_ff1=w_ff1.astype(BF16), w_ff2=w_ff2.astype(BF16), ln2_g=r2(ln2_g), ln2_b=r2(ln2_b),
    )


def _rope_tables(n):
    pos = np.arange(n)
    row = (pos // GRID_W).astype(np.float32)
    col = (pos % GRID_W).astype(np.float32)

    def head_table(d_rot, lead, width):
        d_ax = d_rot // 2
        inv = ROPE_BASE ** (-jnp.arange(0, d_ax, 2, dtype=F32) / d_ax)
        ar = jnp.asarray(row)[:, None] * inv
        ac = jnp.asarray(col)[:, None] * inv
        cos = jnp.concatenate([jnp.cos(ar), jnp.cos(ar), jnp.cos(ac), jnp.cos(ac)], axis=-1)
        sin = jnp.concatenate([jnp.sin(ar), jnp.sin(ar), jnp.sin(ac), jnp.sin(ac)], axis=-1)
        tail = width - lead - d_rot
        cos = jnp.pad(cos, [(0, 0), (lead, tail)], constant_values=1.0)
        sin = jnp.pad(sin, [(0, 0), (lead, tail)])
        ident = (jnp.ones((ROW_TILE, width), F32), jnp.zeros((ROW_TILE, width), F32))
        return jnp.concatenate([cos, ident[0]], axis=0), jnp.concatenate([sin, ident[1]], axis=0)

    cos_h, sin_h = head_table(SWA_HEAD_DIM, 0, SWA_HEAD_DIM)
    cos_h = jnp.concatenate([cos_h] * SWA_KV_HEADS, axis=-1)
    sin_h = jnp.concatenate([sin_h] * SWA_KV_HEADS, axis=-1)
    cos_m, sin_m = head_table(MLA_ROPE, MLA_NOPE, HEAD_PAD)
    return cos_h, sin_h, cos_m, sin_m


def kernel(x, c, ctx, c_ctx, w_ada, b_ada, w_in, pool_w, pool_scale, swa_sink, mla_q_norm,
           mla_w_uq, mla_kv_norm, mla_w_ukv, sgu_norm_g, sgu_norm_b, sgu_w, sgu_b, w_out,
           ln1_g, ln1_b, w_ff1, w_ff2, ln2_g, ln2_b):
    B, n, D = x.shape
    C = ctx.shape[1]
    L = w_in.shape[0]
    assert n % (MLA_TILES_PER_ITER * ROW_TILE) == 0 and MIX_ROWS == ROW_TILE and n % MLA_TQ == 0
    assert MLA_TQ % MLA_QSUB == 0 and ROW_TILE % MLA_KSUB == 0 and C % MLA_KSUB == 0
    assert (B * C) % ROW_TILE == 0 and ROW_TILE % C == 0 and C % BLOCK == 0 and B + 1 <= 8
    alpha = (2 * L) ** 0.25
    n_lat = B * n
    n_lat_tiles = n_lat // ROW_TILE
    n_all_tiles = (n_lat + B * C) // ROW_TILE
    tiles_per_seq = n // ROW_TILE

    cond = jnp.concatenate([c, c_ctx[None, :], jnp.zeros((8 - B - 1, D), F32)], axis=0)
    mods_all = _modulation(cond, w_ada, b_ada)
    mods_all = jnp.pad(mods_all.reshape(L, 8, N_MOD, D),
                       [(0, 0), (0, 0), (0, 8 - N_MOD), (0, 0)])

    W = _prepare_weights(w_in, pool_w, pool_scale, mla_q_norm, mla_w_uq, mla_kv_norm, mla_w_ukv,
                         sgu_norm_g, sgu_norm_b, sgu_w, sgu_b, w_out, ln1_g, ln1_b, w_ff1,
                         w_ff2, ln2_g, ln2_b)
    tabs = _rope_tables(n)

    xs = jnp.concatenate([x.reshape(n_lat, D), ctx.reshape(B * C, D)], axis=0)
    for l in range(L):
        last = l == L - 1
        lw = {k: v[l] for k, v in W.items()}
        mods = mods_all[l]
        sink = swa_sink[l]
        zpool, qs, ks, vs, vst, qm, km, vmt, ysgu = _inproj(
            xs, mods, tabs, lw, n_lat_tiles=n_lat_tiles, tiles_per_seq=tiles_per_seq)
        ypool, yswa = _local_mixers(sink, zpool, qs, ks, vst, lw,
                                    n_batch=B, seq_len=n, ctx_len=C)
        ymla = _mla_latent(qm, km, vmt, n_batch=B, seq_len=n, ctx_len=C)
        if last:
            n_tiles = n_lat_tiles
        else:
            ypool, yswa, ymla = _ctx_mixers(sink, zpool, qs, ks, vs, qm, km, vmt, lw,
                                            (ypool, yswa, ymla), n_batch=B, seq_len=n, ctx_len=C)
            n_tiles = n_all_tiles
        xs = _outffn(xs, (ypool, yswa, ymla, ysgu), mods, lw, n_tiles=n_tiles,
                     n_lat_tiles=n_lat_tiles, tiles_per_seq=tiles_per_seq, alpha=alpha)
    return xs[:n_lat].reshape(B, n, D)
```
